```python
import jax, jax.numpy as jnp
from jax import lax
import numpy as np

D_MODEL = 1024
BATCH = 4
SEQ = 8192
DEPTH = 4

N_MIXERS = 2
N_A_LAYERS = (DEPTH + 1) // 2
N_B_LAYERS = DEPTH // 2
HGRN_EXPAND = 128
HGRN_HEADS = D_MODEL // HGRN_EXPAND
HGRN_HEAD_K = HGRN_EXPAND
HGRN_HEAD_V = D_MODEL // HGRN_HEADS
HGRN_CHUNK = 32
CONV_WIDTH = 31
D_FF = 4 * D_MODEL
DEEPNORM_ALPHA = (2.0 * DEPTH) ** 0.25
DEEPNORM_BETA = (8.0 * DEPTH) ** -0.25
LN_EPS = 1e-5
RMS_EPS = 1e-6
GATE_EPS = 1e-6

kernel_name = "hgrn2_conformer_interleaved_deepnorm"


def layer_norm(x, g, b):
    x32 = x.astype(jnp.float32)
    mu = jnp.mean(x32, axis=-1, keepdims=True)
    var = jnp.mean(jnp.square(x32 - mu), axis=-1, keepdims=True)
    y = (x32 - mu) * lax.rsqrt(var + LN_EPS) * g.astype(jnp.float32) + b.astype(jnp.float32)
    return y.astype(x.dtype)


def chunkwise_gated_recurrence(q, k, v, b):
    C = q.shape[-2]
    causal = jnp.tril(jnp.ones((C, C), dtype=bool))[:, :, None]

    def step(S, inp):
        qc, kc, vc, bc = inp
        diff = bc[..., :, None, :] - bc[..., None, :, :]
        decay = jnp.where(causal, jnp.exp(jnp.where(causal, diff, 0.0)), 0.0)
        scores = jnp.einsum('bhtd,bhsd,bhtsd->bhts', qc, kc, decay)
        o = (jnp.einsum('bhts,bhsv->bhtv', scores, vc)
             + jnp.einsum('bhtd,bhdv->bhtv', qc * jnp.exp(bc), S))
        b_last = bc[..., -1:, :]
        S = (jnp.exp(b_last)[..., 0, :, None] * S
             + jnp.einsum('bhsd,bhsv->bhdv', kc * jnp.exp(b_last - bc), vc))
        return S, o

    S0 = jnp.zeros(q.shape[1:3] + (q.shape[-1], v.shape[-1]), jnp.float32)
    _, o = lax.scan(step, S0, (q, k, v, b))
    return o


def hgrn2_mixer(h, w_in, lb, norm_g, w_out):
    B_, S_, D = h.shape
    H, dk, dv, C = HGRN_HEADS, HGRN_HEAD_K, HGRN_HEAD_V, HGRN_CHUNK
    nC = S_ // C
    proj = h @ w_in
    q, fz, v, g = jnp.split(proj, 4, axis=-1)
    q = jax.nn.silu(q.astype(jnp.float32))
    lb32 = lb.astype(jnp.float32)
    f = lb32 + (1.0 - lb32) * jax.nn.sigmoid(fz.astype(jnp.float32))
    log_f = jnp.log(jnp.maximum(f, GATE_EPS))
    k = 1.0 - f

    def to_chunks(t, d):
        return t.astype(jnp.float32).reshape(B_, nC, C, H, d).transpose(1, 0, 3, 2, 4)

    qc, kc, vc = to_chunks(q, dk), to_chunks(k, dk), to_chunks(v, dv)
    bc = jnp.cumsum(to_chunks(log_f, dk), axis=-2)
    o = chunkwise_gated_recurrence(qc, kc, vc, bc)
    o = o.transpose(1, 0, 3, 2, 4).reshape(B_, S_, H, dv)
    o = o * lax.rsqrt(jnp.mean(jnp.square(o), axis=-1, keepdims=True) + RMS_EPS)
    o = o * norm_g.astype(jnp.float32).reshape(H, dv)
    o = o.reshape(B_, S_, D) * jax.nn.silu(g.astype(jnp.float32))
    return o.astype(h.dtype) @ w_out


def conformer_conv_mixer(h, w_pw1, b_pw1, w_dw, b_dw, ln_g, ln_b, w_pw2, b_pw2):
    u = h @ w_pw1 + b_pw1
    a, gate = jnp.split(u, 2, axis=-1)
    u = a * jax.nn.sigmoid(gate)
    u = lax.conv_general_dilated(
        u, w_dw[:, None, :].astype(u.dtype), window_strides=(1,), padding=[(CONV_WIDTH - 1, 0)],
        dimension_numbers=('NWC', 'WIO', 'NWC'), feature_group_count=D_MODEL) + b_dw
    u = jax.nn.silu(layer_norm(u, ln_g, ln_b))
    return u @ w_pw2 + b_pw2


def setup_inputs(seed: int = 0) -> dict:
    key = jax.random.key(seed)
    ks = jax.random.split(key, 20)
    D, F, K = D_MODEL, D_FF, CONV_WIDTH
    beta = DEEPNORM_BETA

    def nrm(k, shape, scale):
        return jax.random.normal(k, shape, jnp.float32) * scale

    x = nrm(ks[0], (BATCH, SEQ, D), 1.0)
    ln_mix_g = 1.0 + nrm(ks[1], (DEPTH, D), 0.02)
    ln_mix_b = nrm(ks[2], (DEPTH, D), 0.02)
    ln_ffn_g = 1.0 + nrm(ks[3], (DEPTH, D), 0.02)
    ln_ffn_b = nrm(ks[4], (DEPTH, D), 0.02)
    ffn_w1 = nrm(ks[5], (DEPTH, D, F), D ** -0.5 * beta)
    ffn_w2 = nrm(ks[6], (DEPTH, F, D), F ** -0.5 * beta)
    col_scale = jnp.concatenate([jnp.ones((2 * D,), jnp.float32), jnp.full((D,), beta, jnp.float32),
                                 jnp.ones((D,), jnp.float32)])
    a_w_in = nrm(ks[7], (N_A_LAYERS, D, 4 * D), D ** -0.5) * col_scale
    a_lb_logits = nrm(ks[8], (N_A_LAYERS, D), 0.5)
    a_norm_g = 1.0 + nrm(ks[9], (N_A_LAYERS, D), 0.02)
    a_w_out = nrm(ks[10], (N_A_LAYERS, D, D), D ** -0.5 * beta)
    b_w_pw1 = nrm(ks[11], (N_B_LAYERS, D, 2 * D), D ** -0.5)
    b_b_pw1 = nrm(ks[12], (N_B_LAYERS, 2 * D), 0.02)
    b_w_dw = nrm(ks[13], (N_B_LAYERS, K, D), K ** -0.5)
    b_b_dw = nrm(ks[14], (N_B_LAYERS, D), 0.02)
    b_ln_g = 1.0 + nrm(ks[15], (N_B_LAYERS, D), 0.02)
    b_ln_b = nrm(ks[16], (N_B_LAYERS, D), 0.02)
    b_w_pw2 = nrm(ks[17], (N_B_LAYERS, D, D), D ** -0.5 * beta)
    b_b_pw2 = nrm(ks[18], (N_B_LAYERS, D), 0.02)
    return {"x": x, "ln_mix_g": ln_mix_g, "ln_mix_b": ln_mix_b, "ln_ffn_g": ln_ffn_g, "ln_ffn_b": ln_ffn_b,
            "ffn_w1": ffn_w1, "ffn_w2": ffn_w2,
            "a_w_in": a_w_in, "a_lb_logits": a_lb_logits, "a_norm_g": a_norm_g, "a_w_out": a_w_out,
            "b_w_pw1": b_w_pw1, "b_b_pw1": b_b_pw1, "b_w_dw": b_w_dw, "b_b_dw": b_b_dw,
            "b_ln_g": b_ln_g, "b_ln_b": b_ln_b, "b_w_pw2": b_w_pw2, "b_b_pw2": b_b_pw2}


def reference(x, ln_mix_g, ln_mix_b, ln_ffn_g, ln_ffn_b, ffn_w1, ffn_w2,
              a_w_in, a_lb_logits, a_norm_g, a_w_out,
              b_w_pw1, b_b_pw1, b_w_dw, b_b_dw, b_ln_g, b_ln_b, b_w_pw2, b_b_pw2):
    lb_soft = jax.nn.softmax(a_lb_logits.astype(jnp.float32), axis=0)
    lb_all = jnp.cumsum(lb_soft, axis=0) - lb_soft[0]
    for i in range(DEPTH):
        j = i // N_MIXERS
        if i % N_MIXERS == 0:
            mix = hgrn2_mixer(x, a_w_in[j], lb_all[j], a_norm_g[j], a_w_out[j])
        else:
            mix = conformer_conv_mixer(x, b_w_pw1[j], b_b_pw1[j], b_w_dw[j], b_b_dw[j],
                                       b_ln_g[j], b_ln_b[j], b_w_pw2[j], b_b_pw2[j])
        x = layer_norm(DEEPNORM_ALPHA * x + mix, ln_mix_g[i], ln_mix_b[i])
        ff = jnp.square(jax.nn.relu(x @ ffn_w1[i])) @ ffn_w2[i]
        x = layer_norm(DEEPNORM_ALPHA * x + ff, ln_ffn_g[i], ln_ffn_b[i])
    return x
```

```python
import functools

import jax
import jax.numpy as jnp
from jax import lax
from jax.experimental import pallas as pl
from jax.experimental.pallas import tpu as pltpu

D_MODEL = 1024
DEPTH = 4
N_MIXERS = 2
HEADS = 8
HEAD_DIM = 128
CHUNK = 32
CONV_WIDTH = 31
D_FF = 4 * D_MODEL
ALPHA = (2.0 * DEPTH) ** 0.25
LN_EPS = 1e-5
RMS_EPS = 1e-6
GATE_EPS = 1e-6

SUBLANES = 8
LANES = 128
VMEM_LIMIT_BYTES = 56 * 1024 * 1024

MIX_TILE = 256
FFN_TILE = 512
FFN_CHUNK = 1024
CONV_HIST = 32
CONV_ROWS = 64
CONV_COLS = 256

bf16 = jnp.bfloat16
f32 = jnp.float32


def _layer_norm(y, g, b):
    mu = jnp.mean(y, axis=-1, keepdims=True)
    yc = y - mu
    var = jnp.mean(yc * yc, axis=-1, keepdims=True)
    return yc * lax.rsqrt(var + LN_EPS) * g + b


def _dot(a, b):
    return jnp.dot(a, b, preferred_element_type=f32)


def _dot_nt(a, b):
    return lax.dot_general(a, b, (((1,), (1,)), ((), ())), preferred_element_type=f32)


def _dot_tn(a, b):
    return lax.dot_general(a, b, (((0,), (0,)), ((), ())), preferred_element_type=f32)


def _ffn_kernel(x_ref, w1_ref, w2_ref, g_ref, b_ref, o_ref):
    x = x_ref[...]
    xb = x.astype(bf16)
    acc = None
    for c in range(D_FF // FFN_CHUNK):
        cols = slice(c * FFN_CHUNK, (c + 1) * FFN_CHUNK)
        h = _dot(xb, w1_ref[:, cols])
        h = jnp.square(jnp.maximum(h, 0.0)).astype(bf16)
        part = _dot(h, w2_ref[cols, :])
        acc = part if acc is None else acc + part
    o_ref[...] = _layer_norm(ALPHA * x + acc, g_ref[...], b_ref[...])


def _ffn(x2d, w1, w2, g, b):
    n_tok = x2d.shape[0]
    row = lambda i: (i, 0)
    full = lambda i: (0, 0)
    return pl.pallas_call(
        _ffn_kernel,
        out_shape=jax.ShapeDtypeStruct((n_tok, D_MODEL), f32),
        grid=(n_tok // FFN_TILE,),
        in_specs=[
            pl.BlockSpec((FFN_TILE, D_MODEL), row),
            pl.BlockSpec((D_MODEL, D_FF), full),
            pl.BlockSpec((D_FF, D_MODEL), full),
            pl.BlockSpec((1, D_MODEL), full),
            pl.BlockSpec((1, D_MODEL), full),
        ],
        out_specs=pl.BlockSpec((FFN_TILE, D_MODEL), row),
        compiler_params=pltpu.CompilerParams(
            dimension_semantics=("arbitrary",), vmem_limit_bytes=VMEM_LIMIT_BYTES),
        name="ffn",
    )(x2d, w1, w2, g, b)


def _chunk_masks():
    t = lax.broadcasted_iota(jnp.int32, (CHUNK, CHUNK), 0)
    s = lax.broadcasted_iota(jnp.int32, (CHUNK, CHUNK), 1)
    leaf = (t // 4 == s // 4) & (s <= t)
    lvl3 = (t // 8 == s // 8) & (t % 8 >= 4) & (s % 8 < 4)
    lvl2 = (t // 16 == s // 16) & (t % 16 >= 8) & (s % 16 < 8)
    lvl1 = (t >= 16) & (s < 16)
    return leaf, lvl3, lvl2, lvl1


def _hgrn_kernel(layer_j, x_ref, win_ref, lbl_ref, ng_ref, wout_ref, lng_ref, lnb_ref, o_ref,
                 q_s, k_s, v_s, lf_s, gate_s, og_s, st_s):
    D = D_MODEL

    @pl.when(pl.program_id(1) == 0)
    def _():
        st_s[...] = jnp.zeros_like(st_s)

    x = x_ref[...]
    xb = x.astype(bf16)

    logits = lbl_ref[...]
    e = jnp.exp(logits - jnp.max(logits, axis=0, keepdims=True))
    soft = e / jnp.sum(e, axis=0, keepdims=True)
    lb = jnp.zeros((1, D), f32)
    for l in range(1, layer_j + 1):
        lb = lb + soft[l:l + 1, :]

    q = _dot(xb, win_ref[:, 0:D])
    q_s[...] = q * jax.nn.sigmoid(q)
    fz = _dot(xb, win_ref[:, D:2 * D])
    f = lb + (1.0 - lb) * jax.nn.sigmoid(fz)
    lf_s[...] = jnp.log(jnp.maximum(f, GATE_EPS))
    k_s[...] = 1.0 - f
    v_s[...] = _dot(xb, win_ref[:, 2 * D:3 * D]).astype(bf16)
    g = _dot(xb, win_ref[:, 3 * D:4 * D])
    gate_s[...] = g * jax.nn.sigmoid(g)

    sub = lax.broadcasted_iota(jnp.int32, (SUBLANES, D), 0)
    m1 = (sub % 4) >= 1
    m2 = (sub % 4) >= 2
    lo = sub < 4
    mask_leaf, mask3, mask2, mask1 = _chunk_masks()
    ng = ng_ref[...]

    def chunk_body(c, carry):
        r0 = pl.multiple_of(c * CHUNK, CHUNK)
        rows = pl.ds(r0, CHUNK)
        lf = lf_s[rows, :]

        p4, t4, p8, t8 = [], [], [], []
        for r in range(CHUNK // SUBLANES):
            xr = lf[SUBLANES * r:SUBLANES * (r + 1), :]
            s1 = xr + jnp.where(m1, pltpu.roll(xr, 1, 0), 0.0)
            a4 = s1 + jnp.where(m2, pltpu.roll(s1, 2, 0), 0.0)
            b4 = jnp.broadcast_to(a4[3:4, :], (SUBLANES, D))
            a8 = a4 + jnp.where(lo, 0.0, b4)
            b8 = jnp.broadcast_to(a8[7:8, :], (SUBLANES, D))
            p4.append(a4); t4.append(b4); p8.append(a8); t8.append(b8)
        t16 = [t8[0] + t8[1], t8[2] + t8[3]]
        t32 = t16[0] + t16[1]
        p16 = [p8[0], p8[1] + t8[0], p8[2], p8[3] + t8[2]]
        p32 = [p16[0], p16[1], p16[2] + t16[0], p16[3] + t16[0]]

        cat = lambda xs: jnp.concatenate(xs, axis=0)
        pl4 = cat(p4)
        e_leaf_q = jnp.exp(pl4)
        e_leaf_k = jnp.exp(-pl4)
        e3 = jnp.exp(cat([jnp.where(lo, t4[r] - p4[r], p4[r]) for r in range(4)]))
        e2 = jnp.exp(cat([t8[0] - p8[0], p8[1], t8[2] - p8[2], p8[3]]))
        e1 = jnp.exp(cat([t16[0] - p16[0], t16[0] - p16[1], p16[2], p16[3]]))
        ei_q = jnp.exp(cat(p32))
        ei_k = jnp.exp(cat([t32 - p32[r] for r in range(4)]))
        dec = jnp.exp(t32[0:1, :])

        q = q_s[rows, :]
        k = k_s[rows, :]
        v = v_s[rows, :]
        gate = gate_s[rows, :]
        qs = [(q * ee).astype(bf16) for ee in (e_leaf_q, e3, e2, e1, ei_q)]
        ks = [(k * ee).astype(bf16) for ee in (e_leaf_k, e3, e2, e1, ei_k)]

        for h in range(HEADS):
            hs = slice(h * HEAD_DIM, (h + 1) * HEAD_DIM)
            s_leaf = _dot_nt(qs[0][:, hs], ks[0][:, hs])
            s3 = _dot_nt(qs[1][:, hs], ks[1][:, hs])
            s2 = _dot_nt(qs[2][:, hs], ks[2][:, hs])
            s1_ = _dot_nt(qs[3][:, hs], ks[3][:, hs])
            p = jnp.where(mask_leaf, s_leaf,
                          jnp.where(mask3, s3, jnp.where(mask2, s2, jnp.where(mask1, s1_, 0.0))))
            vh = v[:, hs]
            st = st_s[h]
            o = _dot(p.astype(bf16), vh) + _dot_nt(qs[4][:, hs], st.astype(bf16))
            st_s[h] = st * dec[:, hs] + _dot_tn(vh, ks[4][:, hs])
            o = o * lax.rsqrt(jnp.mean(o * o, axis=-1, keepdims=True) + RMS_EPS)
            og_s[rows, hs] = (o * ng[:, hs] * gate[:, hs]).astype(bf16)
        return carry

    lax.fori_loop(0, x.shape[0] // CHUNK, chunk_body, 0)

    mix = _dot(og_s[...], wout_ref[...])
    o_ref[...] = _layer_norm(ALPHA * x + mix, lng_ref[...], lnb_ref[...])


def _hgrn_mixer(x, layer_j, w_in, lb_logits, norm_g, w_out, ln_g, ln_b):
    B, S, D = x.shape
    tt = MIX_TILE
    tile = lambda b, t: (b, t, 0)
    full = lambda b, t: (0, 0)
    n_a = lb_logits.shape[0]
    return pl.pallas_call(
        functools.partial(_hgrn_kernel, layer_j),
        out_shape=jax.ShapeDtypeStruct((B, S, D), f32),
        grid=(B, S // tt),
        in_specs=[
            pl.BlockSpec((None, tt, D), tile),
            pl.BlockSpec((D, 4 * D), full),
            pl.BlockSpec((n_a, D), full),
            pl.BlockSpec((1, D), full),
            pl.BlockSpec((D, D), full),
            pl.BlockSpec((1, D), full),
            pl.BlockSpec((1, D), full),
        ],
        out_specs=pl.BlockSpec((None, tt, D), tile),
        scratch_shapes=[
            pltpu.VMEM((tt, D), f32),
            pltpu.VMEM((tt, D), f32),
            pltpu.VMEM((tt, D), bf16),
            pltpu.VMEM((tt, D), f32),
            pltpu.VMEM((tt, D), f32),
            pltpu.VMEM((tt, D), bf16),
            pltpu.VMEM((HEADS, HEAD_DIM, HEAD_DIM), f32),
        ],
        compiler_params=pltpu.CompilerParams(
            dimension_semantics=("arbitrary", "arbitrary"), vmem_limit_bytes=VMEM_LIMIT_BYTES),
        name="hgrn_mixer",
    )(x, w_in, lb_logits, norm_g, w_out, ln_g, ln_b)


def _conv_kernel(x_ref, w1_ref, b1_ref, wdw_ref, bdw_ref, cg_ref, cb_ref, w2_ref, b2_ref,
                 lng_ref, lnb_ref, o_ref, u_s, c_s):
    D = D_MODEL
    tt = x_ref.shape[0]

    @pl.when(pl.program_id(1) == 0)
    def _():
        u_s[0:CONV_HIST, :] = jnp.zeros((CONV_HIST, D), f32)

    x = x_ref[...]
    xb = x.astype(bf16)
    a = _dot(xb, w1_ref[:, 0:D]) + b1_ref[:, 0:D]
    gate = _dot(xb, w1_ref[:, D:2 * D]) + b1_ref[:, D:2 * D]
    u_s[CONV_HIST:CONV_HIST + tt, :] = a * jax.nn.sigmoid(gate)

    base = CONV_HIST - (CONV_WIDTH - 1)
    for rb in range(tt // CONV_ROWS):
        for cb in range(D // CONV_COLS):
            cols = slice(cb * CONV_COLS, (cb + 1) * CONV_COLS)
            acc = jnp.zeros((CONV_ROWS, CONV_COLS), f32)
            for j in range(CONV_WIDTH):
                r = rb * CONV_ROWS + base + j
                acc = acc + u_s[r:r + CONV_ROWS, cols] * wdw_ref[j:j + 1, cols]
            c_s[rb * CONV_ROWS:(rb + 1) * CONV_ROWS, cols] = acc + bdw_ref[:, cols]

    u_s[0:CONV_HIST, :] = u_s[tt:tt + CONV_HIST, :]

    u = _layer_norm(c_s[...], cg_ref[...], cb_ref[...])
    u = u * jax.nn.sigmoid(u)
    mix = _dot(u.astype(bf16), w2_ref[...]) + b2_ref[...]
    o_ref[...] = _layer_norm(ALPHA * x + mix, lng_ref[...], lnb_ref[...])


def _conv_mixer(x, w1, b1, w_dw, b_dw, cg, cb, w2, b2, ln_g, ln_b):
    B, S, D = x.shape
    tt = MIX_TILE
    tile = lambda b, t: (b, t, 0)
    full = lambda b, t: (0, 0)
    vec = pl.BlockSpec((1, D), full)
    return pl.pallas_call(
        _conv_kernel,
        out_shape=jax.ShapeDtypeStruct((B, S, D), f32),
        grid=(B, S // tt),
        in_specs=[
            pl.BlockSpec((None, tt, D), tile),
            pl.BlockSpec((D, 2 * D), full),
            pl.BlockSpec((1, 2 * D), full),
            pl.BlockSpec((CONV_WIDTH, D), full),
            vec, vec, vec,
            pl.BlockSpec((D, D), full),
            vec, vec, vec,
        ],
        out_specs=pl.BlockSpec((None, tt, D), tile),
        scratch_shapes=[
            pltpu.VMEM((CONV_HIST + tt, D), f32),
            pltpu.VMEM((tt, D), f32),
        ],
        compiler_params=pltpu.CompilerParams(
            dimension_semantics=("arbitrary", "arbitrary"), vmem_limit_bytes=VMEM_LIMIT_BYTES),
        name="conv_mixer",
    )(x, w1, b1, w_dw, b_dw, cg, cb, w2, b2, ln_g, ln_b)


def kernel(x, ln_mix_g, ln_mix_b, ln_ffn_g, ln_ffn_b, ffn_w1, ffn_w2, a_w_in, a_lb_logits, a_norm_g, a_w_out, b_w_pw1, b_b_pw1, b_w_dw, b_b_dw, b_ln_g, b_ln_b, b_w_pw2, b_b_pw2):
    B, S, D = x.shape
    assert D == D_MODEL and S % MIX_TILE == 0 and (B * S) % FFN_TILE == 0
    row = lambda a: a.reshape(1, -1)
    lb_logits = a_lb_logits.astype(f32)
    for i in range(DEPTH):
        j = i // N_MIXERS
        if i % N_MIXERS == 0:
            x = _hgrn_mixer(x, j, a_w_in[j].astype(bf16), lb_logits, row(a_norm_g[j]),
                            a_w_out[j].astype(bf16), row(ln_mix_g[i]), row(ln_mix_b[i]))
        else:
            x = _conv_mixer(x, b_w_pw1[j].astype(bf16), row(b_b_pw1[j]), b_w_dw[j], row(b_b_dw[j]),
                            row(b_ln_g[j]), row(b_ln_b[j]), b_w_pw2[j].astype(bf16), row(b_b_pw2[j]),
                            row(ln_mix_g[i]), row(ln_mix_b[i]))
        x = _ffn(x.reshape(B * S, D), ffn_w1[i].astype(bf16), ffn_w2[i].astype(bf16),
                 row(ln_ffn_g[i]), row(ln_ffn_b[i])).reshape(B, S, D)
    return x
```

```python
import functools

import jax
import jax.numpy as jnp
from jax import lax
from jax.experimental import pallas as pl
from jax.experimental.pallas import tpu as pltpu

D_MODEL = 1024
DEPTH = 4
N_MIXERS = 2
HEADS = 8
HEAD_DIM = 128
CHUNK = 32
CONV_WIDTH = 31
D_FF = 4 * D_MODEL
ALPHA = (2.0 * DEPTH) ** 0.25
LN_EPS = 1e-5
RMS_EPS = 1e-6
GATE_EPS = 1e-6

SUBLANES = 8
LANES = 128
VMEM_LIMIT_BYTES = 56 * 1024 * 1024

MIX_TILE = 256
FFN_TILE = 512
FFN_CHUNK = 1024
CONV_HIST = 32
CONV_ROWS = 64

bf16 = jnp.bfloat16
f32 = jnp.float32


def _layer_norm(y, g, b):
    mu = jnp.mean(y, axis=-1, keepdims=True)
    yc = y - mu
    var = jnp.mean(yc * yc, axis=-1, keepdims=True)
    return yc * lax.rsqrt(var + LN_EPS) * g + b


def _dot(a, b):
    return jnp.dot(a, b, preferred_element_type=f32)


def _dot_nt(a, b):
    return lax.dot_general(a, b, (((1,), (1,)), ((), ())), preferred_element_type=f32)


def _dot_tn(a, b):
    return lax.dot_general(a, b, (((0,), (0,)), ((), ())), preferred_element_type=f32)


def _ffn_kernel(x_ref, w1_ref, w2_ref, g_ref, b_ref, o_ref):
    x = x_ref[...]
    xb = x.astype(bf16)
    acc = None
    for c in range(D_FF // FFN_CHUNK):
        cols = slice(c * FFN_CHUNK, (c + 1) * FFN_CHUNK)
        h = _dot(xb, w1_ref[:, cols])
        h = jnp.square(jnp.maximum(h, 0.0)).astype(bf16)
        part = _dot(h, w2_ref[cols, :])
        acc = part if acc is None else acc + part
    o_ref[...] = _layer_norm(ALPHA * x + acc, g_ref[...], b_ref[...])


def _ffn(x2d, w1, w2, g, b):
    n_tok = x2d.shape[0]
    row = lambda i: (i, 0)
    full = lambda i: (0, 0)
    return pl.pallas_call(
        _ffn_kernel,
        out_shape=jax.ShapeDtypeStruct((n_tok, D_MODEL), f32),
        grid=(n_tok // FFN_TILE,),
        in_specs=[
            pl.BlockSpec((FFN_TILE, D_MODEL), row),
            pl.BlockSpec((D_MODEL, D_FF), full),
            pl.BlockSpec((D_FF, D_MODEL), full),
            pl.BlockSpec((1, D_MODEL), full),
            pl.BlockSpec((1, D_MODEL), full),
        ],
        out_specs=pl.BlockSpec((FFN_TILE, D_MODEL), row),
        compiler_params=pltpu.CompilerParams(
            dimension_semantics=("arbitrary",), vmem_limit_bytes=VMEM_LIMIT_BYTES),
        name="ffn",
    )(x2d, w1, w2, g, b)


def _chunk_masks():
    t = lax.broadcasted_iota(jnp.int32, (CHUNK, CHUNK), 0)
    s = lax.broadcasted_iota(jnp.int32, (CHUNK, CHUNK), 1)
    leaf = (t // 4 == s // 4) & (s <= t)
    lvl3 = (t // 8 == s // 8) & (t % 8 >= 4) & (s % 8 < 4)
    lvl2 = (t // 16 == s // 16) & (t % 16 >= 8) & (s % 16 < 8)
    lvl1 = (t >= 16) & (s < 16)
    return leaf, lvl3, lvl2, lvl1


def _hgrn_kernel(layer_j, x_ref, win_ref, lbl_ref, ng_ref, wout_ref, lng_ref, lnb_ref, o_ref,
                 q_s, k_s, v_s, lf_s, gate_s, og_s, st_s):
    D = D_MODEL

    @pl.when(pl.program_id(1) == 0)
    def _():
        st_s[...] = jnp.zeros_like(st_s)

    x = x_ref[...]
    xb = x.astype(bf16)

    logits = lbl_ref[...]
    e = jnp.exp(logits - jnp.max(logits, axis=0, keepdims=True))
    soft = e / jnp.sum(e, axis=0, keepdims=True)
    lb = jnp.zeros((1, D), f32)
    for l in range(1, layer_j + 1):
        lb = lb + soft[l:l + 1, :]

    q = _dot(xb, win_ref[:, 0:D])
    q_s[...] = q * jax.nn.sigmoid(q)
    fz = _dot(xb, win_ref[:, D:2 * D])
    f = lb + (1.0 - lb) * jax.nn.sigmoid(fz)
    lf_s[...] = jnp.log(jnp.maximum(f, GATE_EPS))
    k_s[...] = 1.0 - f
    v_s[...] = _dot(xb, win_ref[:, 2 * D:3 * D]).astype(bf16)
    g = _dot(xb, win_ref[:, 3 * D:4 * D])
    gate_s[...] = g * jax.nn.sigmoid(g)

    sub = lax.broadcasted_iota(jnp.int32, (SUBLANES, D), 0)
    m1 = (sub % 4) >= 1
    m2 = (sub % 4) >= 2
    lo = sub < 4
    mask_leaf, mask3, mask2, mask1 = _chunk_masks()
    ng = ng_ref[...]

    def chunk_body(c, carry):
        r0 = pl.multiple_of(c * CHUNK, CHUNK)
        rows = pl.ds(r0, CHUNK)
        lf = lf_s[rows, :]

        p4, t4, p8, t8 = [], [], [], []
        for r in range(CHUNK // SUBLANES):
            xr = lf[SUBLANES * r:SUBLANES * (r + 1), :]
            s1 = xr + jnp.where(m1, pltpu.roll(xr, 1, 0), 0.0)
            a4 = s1 + jnp.where(m2, pltpu.roll(s1, 2, 0), 0.0)
            b4 = jnp.broadcast_to(a4[3:4, :], (SUBLANES, D))
            a8 = a4 + jnp.where(lo, 0.0, b4)
            b8 = jnp.broadcast_to(a8[7:8, :], (SUBLANES, D))
            p4.append(a4); t4.append(b4); p8.append(a8); t8.append(b8)
        t16 = [t8[0] + t8[1], t8[2] + t8[3]]
        t32 = t16[0] + t16[1]
        p16 = [p8[0], p8[1] + t8[0], p8[2], p8[3] + t8[2]]
        p32 = [p16[0], p16[1], p16[2] + t16[0], p16[3] + t16[0]]

        cat = lambda xs: jnp.concatenate(xs, axis=0)
        pl4 = cat(p4)
        e_leaf_q = jnp.exp(pl4)
        e_leaf_k = jnp.exp(-pl4)
        e3 = jnp.exp(cat([jnp.where(lo, t4[r] - p4[r], p4[r]) for r in range(4)]))
        e2 = jnp.exp(cat([t8[0] - p8[0], p8[1], t8[2] - p8[2], p8[3]]))
        e1 = jnp.exp(cat([t16[0] - p16[0], t16[0] - p16[1], p16[2], p16[3]]))
        ei_q = jnp.exp(cat(p32))
        ei_k = jnp.exp(cat([t32 - p32[r] for r in range(4)]))
        dec = jnp.exp(t32[0:1, :])

        q = q_s[rows, :]
        k = k_s[rows, :]
        v = v_s[rows, :]
        gate = gate_s[rows, :]
        qs = [(q * ee).astype(bf16) for ee in (e_leaf_q, e3, e2, e1, ei_q)]
        ks = [(k * ee).astype(bf16) for ee in (e_leaf_k, e3, e2, e1, ei_k)]

        for h in range(HEADS):
            hs = slice(h * HEAD_DIM, (h + 1) * HEAD_DIM)
            s_leaf = _dot_nt(qs[0][:, hs], ks[0][:, hs])
            s3 = _dot_nt(qs[1][:, hs], ks[1][:, hs])
            s2 = _dot_nt(qs[2][:, hs], ks[2][:, hs])
            s1_ = _dot_nt(qs[3][:, hs], ks[3][:, hs])
            p = jnp.where(mask_leaf, s_leaf,
                          jnp.where(mask3, s3, jnp.where(mask2, s2, jnp.where(mask1, s1_, 0.0))))
            vh = v[:, hs]
            st = st_s[h]
            o = _dot(p.astype(bf16), vh) + _dot_nt(qs[4][:, hs], st.astype(bf16))
            st_s[h] = st * dec[:, hs] + _dot_tn(vh, ks[4][:, hs])
            o = o * lax.rsqrt(jnp.mean(o * o, axis=-1, keepdims=True) + RMS_EPS)
            og_s[rows, hs] = (o * ng[:, hs] * gate[:, hs]).astype(bf16)
        return carry

    lax.fori_loop(0, x.shape[0] // CHUNK, chunk_body, 0)

    mix = _dot(og_s[...], wout_ref[...])
    o_ref[...] = _layer_norm(ALPHA * x + mix, lng_ref[...], lnb_ref[...])


def _hgrn_mixer(x, layer_j, w_in, lb_logits, norm_g, w_out, ln_g, ln_b):
    B, S, D = x.shape
    tt = MIX_TILE
    tile = lambda b, t: (b, t, 0)
    full = lambda b, t: (0, 0)
    n_a = lb_logits.shape[0]
    return pl.pallas_call(
        functools.partial(_hgrn_kernel, layer_j),
        out_shape=jax.ShapeDtypeStruct((B, S, D), f32),
        grid=(B, S // tt),
        in_specs=[
            pl.BlockSpec((None, tt, D), tile),
            pl.BlockSpec((D, 4 * D), full),
            pl.BlockSpec((n_a, D), full),
            pl.BlockSpec((1, D), full),
            pl.BlockSpec((D, D), full),
            pl.BlockSpec((1, D), full),
            pl.BlockSpec((1, D), full),
        ],
        out_specs=pl.BlockSpec((None, tt, D), tile),
        scratch_shapes=[
            pltpu.VMEM((tt, D), f32),
            pltpu.VMEM((tt, D), f32),
            pltpu.VMEM((tt, D), bf16),
            pltpu.VMEM((tt, D), f32),
            pltpu.VMEM((tt, D), f32),
            pltpu.VMEM((tt, D), bf16),
            pltpu.VMEM((HEADS, HEAD_DIM, HEAD_DIM), f32),
        ],
        compiler_params=pltpu.CompilerParams(
            dimension_semantics=("arbitrary", "arbitrary"), vmem_limit_bytes=VMEM_LIMIT_BYTES),
        name="hgrn_mixer",
    )(x, w_in, lb_logits, norm_g, w_out, ln_g, ln_b)


def _conv_kernel(x_ref, w1_ref, b1_ref, wdw_ref, bdw_ref, cg_ref, cb_ref, w2_ref, b2_ref,
                 lng_ref, lnb_ref, o_ref, u_s, c_s):
    D = D_MODEL
    tt = x_ref.shape[0]
    n_slabs = D // LANES

    @pl.when(pl.program_id(1) == 0)
    def _():
        u_s[:, 0:CONV_HIST, :] = jnp.zeros((n_slabs, CONV_HIST, LANES), f32)

    x = x_ref[...]
    xb = x.astype(bf16)
    a = _dot(xb, w1_ref[:, 0:D]) + b1_ref[:, 0:D]
    gate = _dot(xb, w1_ref[:, D:2 * D]) + b1_ref[:, D:2 * D]
    glu = a * jax.nn.sigmoid(gate)
    for s in range(n_slabs):
        u_s[s, CONV_HIST:CONV_HIST + tt, :] = glu[:, s * LANES:(s + 1) * LANES]

    base = CONV_HIST - (CONV_WIDTH - 1)
    for s in range(n_slabs):
        cols = slice(s * LANES, (s + 1) * LANES)
        for rb in range(tt // CONV_ROWS):
            acc = jnp.zeros((CONV_ROWS, LANES), f32)
            for j in range(CONV_WIDTH):
                r = rb * CONV_ROWS + base + j
                acc = acc + u_s[s, r:r + CONV_ROWS, :] * wdw_ref[j:j + 1, cols]
            c_s[rb * CONV_ROWS:(rb + 1) * CONV_ROWS, cols] = acc + bdw_ref[:, cols]

    u_s[:, 0:CONV_HIST, :] = u_s[:, tt:tt + CONV_HIST, :]

    u = _layer_norm(c_s[...], cg_ref[...], cb_ref[...])
    u = u * jax.nn.sigmoid(u)
    mix = _dot(u.astype(bf16), w2_ref[...]) + b2_ref[...]
    o_ref[...] = _layer_norm(ALPHA * x + mix, lng_ref[...], lnb_ref[...])


def _conv_mixer(x, w1, b1, w_dw, b_dw, cg, cb, w2, b2, ln_g, ln_b):
    B, S, D = x.shape
    tt = MIX_TILE
    tile = lambda b, t: (b, t, 0)
    full = lambda b, t: (0, 0)
    vec = pl.BlockSpec((1, D), full)
    return pl.pallas_call(
        _conv_kernel,
        out_shape=jax.ShapeDtypeStruct((B, S, D), f32),
        grid=(B, S // tt),
        in_specs=[
            pl.BlockSpec((None, tt, D), tile),
            pl.BlockSpec((D, 2 * D), full),
            pl.BlockSpec((1, 2 * D), full),
            pl.BlockSpec((CONV_WIDTH, D), full),
            vec, vec, vec,
            pl.BlockSpec((D, D), full),
            vec, vec, vec,
        ],
        out_specs=pl.BlockSpec((None, tt, D), tile),
        scratch_shapes=[
            pltpu.VMEM((D // LANES, CONV_HIST + tt, LANES), f32),
            pltpu.VMEM((tt, D), f32),
        ],
        compiler_params=pltpu.CompilerParams(
            dimension_semantics=("arbitrary", "arbitrary"), vmem_limit_bytes=VMEM_LIMIT_BYTES),
        name="conv_mixer",
    )(x, w1, b1, w_dw, b_dw, cg, cb, w2, b2, ln_g, ln_b)


def kernel(x, ln_mix_g, ln_mix_b, ln_ffn_g, ln_ffn_b, ffn_w1, ffn_w2, a_w_in, a_lb_logits, a_norm_g, a_w_out, b_w_pw1, b_b_pw1, b_w_dw, b_b_dw, b_ln_g, b_ln_b, b_w_pw2, b_b_pw2):
    B, S, D = x.shape
    assert D == D_MODEL and S % MIX_TILE == 0 and (B * S) % FFN_TILE == 0
    row = lambda a: a.reshape(1, -1)
    lb_logits = a_lb_logits.astype(f32)
    for i in range(DEPTH):
        j = i // N_MIXERS
        if i % N_MIXERS == 0:
            x = _hgrn_mixer(x, j, a_w_in[j].astype(bf16), lb_logits, row(a_norm_g[j]),
                            a_w_out[j].astype(bf16), row(ln_mix_g[i]), row(ln_mix_b[i]))
        else:
            x = _conv_mixer(x, b_w_pw1[j].astype(bf16), row(b_b_pw1[j]), b_w_dw[j], row(b_b_dw[j]),
                            row(b_ln_g[j]), row(b_ln_b[j]), b_w_pw2[j].astype(bf16), row(b_b_pw2[j]),
                            row(ln_mix_g[i]), row(ln_mix_b[i]))
        x = _ffn(x.reshape(B * S, D), ffn_w1[i].astype(bf16), ffn_w2[i].astype(bf16),
                 row(ln_ffn_g[i]), row(ln_ffn_b[i])).reshape(B, S, D)
    return x
```

```python
import functools

import jax
import jax.numpy as jnp
from jax import lax
from jax.experimental import pallas as pl
from jax.experimental.pallas import tpu as pltpu

D_MODEL = 1024
DEPTH = 4
N_MIXERS = 2
HEADS = 8
HEAD_DIM = 128
CHUNK = 32
CONV_WIDTH = 31
D_FF = 4 * D_MODEL
ALPHA = (2.0 * DEPTH) ** 0.25
LN_EPS = 1e-5
RMS_EPS = 1e-6
GATE_EPS = 1e-6

SUBLANES = 8
LANES = 128
VMEM_LIMIT_BYTES = 56 * 1024 * 1024

MIX_TILE = 256
FFN_TILE = 512
FFN_CHUNK = 1024
CONV_HIST = 32
CONV_ROWS = 64

bf16 = jnp.bfloat16
f32 = jnp.float32


def _layer_norm(y, g, b):
    mu = jnp.mean(y, axis=-1, keepdims=True)
    yc = y - mu
    var = jnp.mean(yc * yc, axis=-1, keepdims=True)
    return yc * lax.rsqrt(var + LN_EPS) * g + b


def _dot(a, b):
    return jnp.dot(a, b, preferred_element_type=f32)


def _dot_nt(a, b):
    return lax.dot_general(a, b, (((1,), (1,)), ((), ())), preferred_element_type=f32)


def _dot_tn(a, b):
    return lax.dot_general(a, b, (((0,), (0,)), ((), ())), preferred_element_type=f32)


def _ffn_kernel(x_ref, w1_ref, w2_ref, g_ref, b_ref, o_ref):
    x = x_ref[...]
    xb = x.astype(bf16)
    acc = None
    for c in range(D_FF // FFN_CHUNK):
        cols = slice(c * FFN_CHUNK, (c + 1) * FFN_CHUNK)
        h = _dot(xb, w1_ref[:, cols])
        h = jnp.square(jnp.maximum(h, 0.0)).astype(bf16)
        part = _dot(h, w2_ref[cols, :])
        acc = part if acc is None else acc + part
    o_ref[...] = _layer_norm(ALPHA * x + acc, g_ref[...], b_ref[...])


def _ffn(x2d, w1, w2, g, b):
    n_tok = x2d.shape[0]
    row = lambda i: (i, 0)
    full = lambda i: (0, 0)
    return pl.pallas_call(
        _ffn_kernel,
        out_shape=jax.ShapeDtypeStruct((n_tok, D_MODEL), f32),
        grid=(n_tok // FFN_TILE,),
        in_specs=[
            pl.BlockSpec((FFN_TILE, D_MODEL), row),
            pl.BlockSpec((D_MODEL, D_FF), full),
            pl.BlockSpec((D_FF, D_MODEL), full),
            pl.BlockSpec((1, D_MODEL), full),
            pl.BlockSpec((1, D_MODEL), full),
        ],
        out_specs=pl.BlockSpec((FFN_TILE, D_MODEL), row),
        compiler_params=pltpu.CompilerParams(
            dimension_semantics=("arbitrary",), vmem_limit_bytes=VMEM_LIMIT_BYTES),
        name="ffn",
    )(x2d, w1, w2, g, b)


def _chunk_masks():
    t = lax.broadcasted_iota(jnp.int32, (CHUNK, CHUNK), 0)
    s = lax.broadcasted_iota(jnp.int32, (CHUNK, CHUNK), 1)
    leaf = (t // 4 == s // 4) & (s <= t)
    lvl3 = (t // 8 == s // 8) & (t % 8 >= 4) & (s % 8 < 4)
    lvl2 = (t // 16 == s // 16) & (t % 16 >= 8) & (s % 16 < 8)
    lvl1 = (t >= 16) & (s < 16)
    return leaf, lvl3, lvl2, lvl1


def _hgrn_kernel(layer_j, x_ref, win_ref, lbl_ref, ng_ref, wout_ref, lng_ref, lnb_ref, o_ref,
                 q_s, k_s, v_s, lf_s, gate_s, qs_s, ks_s, dec_s, o_s, og_s, st_s):
    D = D_MODEL

    @pl.when(pl.program_id(1) == 0)
    def _():
        st_s[...] = jnp.zeros_like(st_s)

    x = x_ref[...]
    xb = x.astype(bf16)

    logits = lbl_ref[...]
    e = jnp.exp(logits - jnp.max(logits, axis=0, keepdims=True))
    soft = e / jnp.sum(e, axis=0, keepdims=True)
    lb = jnp.zeros((1, D), f32)
    for l in range(1, layer_j + 1):
        lb = lb + soft[l:l + 1, :]

    q = _dot(xb, win_ref[:, 0:D])
    q_s[...] = q * jax.nn.sigmoid(q)
    fz = _dot(xb, win_ref[:, D:2 * D])
    f = lb + (1.0 - lb) * jax.nn.sigmoid(fz)
    lf_s[...] = jnp.log(jnp.maximum(f, GATE_EPS))
    k_s[...] = 1.0 - f
    v_s[...] = _dot(xb, win_ref[:, 2 * D:3 * D]).astype(bf16)
    g = _dot(xb, win_ref[:, 3 * D:4 * D])
    gate_s[...] = g * jax.nn.sigmoid(g)

    sub = lax.broadcasted_iota(jnp.int32, (SUBLANES, D), 0)
    m1 = (sub % 4) >= 1
    m2 = (sub % 4) >= 2
    lo = sub < 4
    cat = lambda xs: jnp.concatenate(xs, axis=0)
    n_chunks = x.shape[0] // CHUNK

    for c in range(n_chunks):
        rows = slice(c * CHUNK, (c + 1) * CHUNK)
        lf = lf_s[rows, :]
        p4, t4, p8, t8 = [], [], [], []
        for r in range(CHUNK // SUBLANES):
            xr = lf[SUBLANES * r:SUBLANES * (r + 1), :]
            s1 = xr + jnp.where(m1, pltpu.roll(xr, 1, 0), 0.0)
            a4 = s1 + jnp.where(m2, pltpu.roll(s1, 2, 0), 0.0)
            b4 = jnp.broadcast_to(a4[3:4, :], (SUBLANES, D))
            a8 = a4 + jnp.where(lo, 0.0, b4)
            b8 = jnp.broadcast_to(a8[7:8, :], (SUBLANES, D))
            p4.append(a4); t4.append(b4); p8.append(a8); t8.append(b8)
        t16 = [t8[0] + t8[1], t8[2] + t8[3]]
        t32 = t16[0] + t16[1]
        p16 = [p8[0], p8[1] + t8[0], p8[2], p8[3] + t8[2]]
        p32 = [p16[0], p16[1], p16[2] + t16[0], p16[3] + t16[0]]

        pl4 = cat(p4)
        e3 = jnp.exp(cat([jnp.where(lo, t4[r] - p4[r], p4[r]) for r in range(4)]))
        e2 = jnp.exp(cat([t8[0] - p8[0], p8[1], t8[2] - p8[2], p8[3]]))
        e1 = jnp.exp(cat([t16[0] - p16[0], t16[0] - p16[1], p16[2], p16[3]]))
        q = q_s[rows, :]
        k = k_s[rows, :]
        for l, ee in enumerate((jnp.exp(pl4), e3, e2, e1, jnp.exp(cat(p32)))):
            qs_s[l, rows, :] = (q * ee).astype(bf16)
        for l, ee in enumerate((jnp.exp(-pl4), e3, e2, e1, jnp.exp(cat([t32 - p32[r] for r in range(4)])))):
            ks_s[l, rows, :] = (k * ee).astype(bf16)
        dec_s[c] = jnp.exp(t32)

    mask_leaf, mask3, mask2, mask1 = _chunk_masks()
    heads = [slice(h * HEAD_DIM, (h + 1) * HEAD_DIM) for h in range(HEADS)]

    for c in range(n_chunks):
        rows = slice(c * CHUNK, (c + 1) * CHUNK)
        sc = [[_dot_nt(qs_s[l, rows, hs], ks_s[l, rows, hs]) for l in range(4)] for hs in heads]
        sts = [st_s[h] for h in range(HEADS)]
        o_inter = [_dot_nt(qs_s[4, rows, hs], sts[h].astype(bf16)) for h, hs in enumerate(heads)]
        upd = [_dot_tn(v_s[rows, hs], ks_s[4, rows, hs]) for hs in heads]
        for h, hs in enumerate(heads):
            st_s[h] = sts[h] * dec_s[c, 0:1, hs] + upd[h]
        ps = [jnp.where(mask_leaf, s[0], jnp.where(mask3, s[1], jnp.where(mask2, s[2],
                        jnp.where(mask1, s[3], 0.0)))).astype(bf16) for s in sc]
        for h, hs in enumerate(heads):
            o_s[rows, hs] = _dot(ps[h], v_s[rows, hs]) + o_inter[h]

    for h, hs in enumerate(heads):
        o = o_s[:, hs]
        o = o * lax.rsqrt(jnp.mean(o * o, axis=-1, keepdims=True) + RMS_EPS)
        og_s[:, hs] = (o * ng_ref[:, hs] * gate_s[:, hs]).astype(bf16)

    mix = _dot(og_s[...], wout_ref[...])
    o_ref[...] = _layer_norm(ALPHA * x + mix, lng_ref[...], lnb_ref[...])


def _hgrn_mixer(x, layer_j, w_in, lb_logits, norm_g, w_out, ln_g, ln_b):
    B, S, D = x.shape
    tt = MIX_TILE
    tile = lambda b, t: (b, t, 0)
    full = lambda b, t: (0, 0)
    n_a = lb_logits.shape[0]
    return pl.pallas_call(
        functools.partial(_hgrn_kernel, layer_j),
        out_shape=jax.ShapeDtypeStruct((B, S, D), f32),
        grid=(B, S // tt),
        in_specs=[
            pl.BlockSpec((None, tt, D), tile),
            pl.BlockSpec((D, 4 * D), full),
            pl.BlockSpec((n_a, D), full),
            pl.BlockSpec((1, D), full),
            pl.BlockSpec((D, D), full),
            pl.BlockSpec((1, D), full),
            pl.BlockSpec((1, D), full),
        ],
        out_specs=pl.BlockSpec((None, tt, D), tile),
        scratch_shapes=[
            pltpu.VMEM((tt, D), f32),
            pltpu.VMEM((tt, D), f32),
            pltpu.VMEM((tt, D), bf16),
            pltpu.VMEM((tt, D), f32),
            pltpu.VMEM((tt, D), f32),
            pltpu.VMEM((5, tt, D), bf16),
            pltpu.VMEM((5, tt, D), bf16),
            pltpu.VMEM((tt // CHUNK, SUBLANES, D), f32),
            pltpu.VMEM((tt, D), f32),
            pltpu.VMEM((tt, D), bf16),
            pltpu.VMEM((HEADS, HEAD_DIM, HEAD_DIM), f32),
        ],
        compiler_params=pltpu.CompilerParams(
            dimension_semantics=("arbitrary", "arbitrary"), vmem_limit_bytes=VMEM_LIMIT_BYTES),
        name="hgrn_mixer",
    )(x, w_in, lb_logits, norm_g, w_out, ln_g, ln_b)


def _conv_kernel(x_ref, w1_ref, b1_ref, wdw_ref, bdw_ref, cg_ref, cb_ref, w2_ref, b2_ref,
                 lng_ref, lnb_ref, o_ref, u_s, c_s):
    D = D_MODEL
    tt = x_ref.shape[0]
    n_slabs = D // LANES

    @pl.when(pl.program_id(1) == 0)
    def _():
        u_s[:, 0:CONV_HIST, :] = jnp.zeros((n_slabs, CONV_HIST, LANES), f32)

    x = x_ref[...]
    xb = x.astype(bf16)
    a = _dot(xb, w1_ref[:, 0:D]) + b1_ref[:, 0:D]
    gate = _dot(xb, w1_ref[:, D:2 * D]) + b1_ref[:, D:2 * D]
    glu = a * jax.nn.sigmoid(gate)
    for s in range(n_slabs):
        u_s[s, CONV_HIST:CONV_HIST + tt, :] = glu[:, s * LANES:(s + 1) * LANES]

    base = CONV_HIST - (CONV_WIDTH - 1)
    for s in range(n_slabs):
        cols = slice(s * LANES, (s + 1) * LANES)
        for rb in range(tt // CONV_ROWS):
            acc = jnp.zeros((CONV_ROWS, LANES), f32)
            for j in range(CONV_WIDTH):
                r = rb * CONV_ROWS + base + j
                acc = acc + u_s[s, r:r + CONV_ROWS, :] * wdw_ref[j:j + 1, cols]
            c_s[rb * CONV_ROWS:(rb + 1) * CONV_ROWS, cols] = acc + bdw_ref[:, cols]

    u_s[:, 0:CONV_HIST, :] = u_s[:, tt:tt + CONV_HIST, :]

    u = _layer_norm(c_s[...], cg_ref[...], cb_ref[...])
    u = u * jax.nn.sigmoid(u)
    mix = _dot(u.astype(bf16), w2_ref[...]) + b2_ref[...]
    o_ref[...] = _layer_norm(ALPHA * x + mix, lng_ref[...], lnb_ref[...])


def _conv_mixer(x, w1, b1, w_dw, b_dw, cg, cb, w2, b2, ln_g, ln_b):
    B, S, D = x.shape
    tt = MIX_TILE
    tile = lambda b, t: (b, t, 0)
    full = lambda b, t: (0, 0)
    vec = pl.BlockSpec((1, D), full)
    return pl.pallas_call(
        _conv_kernel,
        out_shape=jax.ShapeDtypeStruct((B, S, D), f32),
        grid=(B, S // tt),
        in_specs=[
            pl.BlockSpec((None, tt, D), tile),
            pl.BlockSpec((D, 2 * D), full),
            pl.BlockSpec((1, 2 * D), full),
            pl.BlockSpec((CONV_WIDTH, D), full),
            vec, vec, vec,
            pl.BlockSpec((D, D), full),
            vec, vec, vec,
        ],
        out_specs=pl.BlockSpec((None, tt, D), tile),
        scratch_shapes=[
            pltpu.VMEM((D // LANES, CONV_HIST + tt, LANES), f32),
            pltpu.VMEM((tt, D), f32),
        ],
        compiler_params=pltpu.CompilerParams(
            dimension_semantics=("arbitrary", "arbitrary"), vmem_limit_bytes=VMEM_LIMIT_BYTES),
        name="conv_mixer",
    )(x, w1, b1, w_dw, b_dw, cg, cb, w2, b2, ln_g, ln_b)


def kernel(x, ln_mix_g, ln_mix_b, ln_ffn_g, ln_ffn_b, ffn_w1, ffn_w2, a_w_in, a_lb_logits, a_norm_g, a_w_out, b_w_pw1, b_b_pw1, b_w_dw, b_b_dw, b_ln_g, b_ln_b, b_w_pw2, b_b_pw2):
    B, S, D = x.shape
    assert D == D_MODEL and S % MIX_TILE == 0 and (B * S) % FFN_TILE == 0
    row = lambda a: a.reshape(1, -1)
    lb_logits = a_lb_logits.astype(f32)
    for i in range(DEPTH):
        j = i // N_MIXERS
        if i % N_MIXERS == 0:
            x = _hgrn_mixer(x, j, a_w_in[j].astype(bf16), lb_logits, row(a_norm_g[j]),
                            a_w_out[j].astype(bf16), row(ln_mix_g[i]), row(ln_mix_b[i]))
        else:
            x = _conv_mixer(x, b_w_pw1[j].astype(bf16), row(b_b_pw1[j]), b_w_dw[j], row(b_b_dw[j]),
                            row(b_ln_g[j]), row(b_ln_b[j]), b_w_pw2[j].astype(bf16), row(b_b_pw2[j]),
                            row(ln_mix_g[i]), row(ln_mix_b[i]))
        x = _ffn(x.reshape(B * S, D), ffn_w1[i].astype(bf16), ffn_w2[i].astype(bf16),
                 row(ln_ffn_g[i]), row(ln_ffn_b[i])).reshape(B, S, D)
    return x
```

```python
import functools

import jax
import jax.numpy as jnp
from jax import lax
from jax.experimental import pallas as pl
from jax.experimental.pallas import tpu as pltpu

D_MODEL = 1024
DEPTH = 4
N_MIXERS = 2
HEADS = 8
HEAD_DIM = 128
CHUNK = 32
CONV_WIDTH = 31
D_FF = 4 * D_MODEL
ALPHA = (2.0 * DEPTH) ** 0.25
LN_EPS = 1e-5
RMS_EPS = 1e-6
GATE_EPS = 1e-6

SUBLANES = 8
LANES = 128
VMEM_LIMIT_BYTES = 56 * 1024 * 1024

MIX_TILE = 256
FFN_TILE = 1024
FFN_CHUNK = 1024
CONV_HIST = 32
CONV_ROWS = 64

bf16 = jnp.bfloat16
f32 = jnp.float32


def _layer_norm(y, g, b):
    mu = jnp.mean(y, axis=-1, keepdims=True)
    yc = y - mu
    var = jnp.mean(yc * yc, axis=-1, keepdims=True)
    return yc * lax.rsqrt(var + LN_EPS) * g + b


def _dot(a, b):
    return jnp.dot(a, b, preferred_element_type=f32)


def _dot_nt(a, b):
    return lax.dot_general(a, b, (((1,), (1,)), ((), ())), preferred_element_type=f32)


def _dot_tn(a, b):
    return lax.dot_general(a, b, (((0,), (0,)), ((), ())), preferred_element_type=f32)


def _ffn_kernel(x_ref, w1_ref, w2_ref, g_ref, b_ref, o_ref):
    x = x_ref[...]
    xb = x.astype(bf16)
    acc = None
    for c in range(D_FF // FFN_CHUNK):
        cols = slice(c * FFN_CHUNK, (c + 1) * FFN_CHUNK)
        h = _dot(xb, w1_ref[:, cols])
        h = jnp.square(jnp.maximum(h, 0.0)).astype(bf16)
        part = _dot(h, w2_ref[cols, :])
        acc = part if acc is None else acc + part
    o_ref[...] = _layer_norm(ALPHA * x + acc, g_ref[...], b_ref[...])


def _ffn(x2d, w1, w2, g, b):
    n_tok = x2d.shape[0]
    row = lambda i: (i, 0)
    full = lambda i: (0, 0)
    return pl.pallas_call(
        _ffn_kernel,
        out_shape=jax.ShapeDtypeStruct((n_tok, D_MODEL), f32),
        grid=(n_tok // FFN_TILE,),
        in_specs=[
            pl.BlockSpec((FFN_TILE, D_MODEL), row),
            pl.BlockSpec((D_MODEL, D_FF), full, pipeline_mode=pl.Buffered(1)),
            pl.BlockSpec((D_FF, D_MODEL), full, pipeline_mode=pl.Buffered(1)),
            pl.BlockSpec((1, D_MODEL), full),
            pl.BlockSpec((1, D_MODEL), full),
        ],
        out_specs=pl.BlockSpec((FFN_TILE, D_MODEL), row),
        compiler_params=pltpu.CompilerParams(
            dimension_semantics=("arbitrary",), vmem_limit_bytes=VMEM_LIMIT_BYTES),
        name="ffn",
    )(x2d, w1, w2, g, b)


def _chunk_masks():
    t = lax.broadcasted_iota(jnp.int32, (CHUNK, CHUNK), 0)
    s = lax.broadcasted_iota(jnp.int32, (CHUNK, CHUNK), 1)
    leaf = (t // 4 == s // 4) & (s <= t)
    lvl3 = (t // 8 == s // 8) & (t % 8 >= 4) & (s % 8 < 4)
    lvl2 = (t // 16 == s // 16) & (t % 16 >= 8) & (s % 16 < 8)
    lvl1 = (t >= 16) & (s < 16)
    return leaf, lvl3, lvl2, lvl1


def _hgrn_kernel(layer_j, x_ref, win_ref, lbl_ref, ng_ref, wout_ref, lng_ref, lnb_ref, o_ref,
                 q_s, k_s, v_s, lf_s, gate_s, qs_s, ks_s, dec_s, o_s, og_s, st_s):
    D = D_MODEL

    @pl.when(pl.program_id(1) == 0)
    def _():
        st_s[...] = jnp.zeros_like(st_s)

    x = x_ref[...]
    xb = x.astype(bf16)

    logits = lbl_ref[...]
    e = jnp.exp(logits - jnp.max(logits, axis=0, keepdims=True))
    soft = e / jnp.sum(e, axis=0, keepdims=True)
    lb = jnp.zeros((1, D), f32)
    for l in range(1, layer_j + 1):
        lb = lb + soft[l:l + 1, :]

    q = _dot(xb, win_ref[:, 0:D])
    q_s[...] = q * jax.nn.sigmoid(q)
    fz = _dot(xb, win_ref[:, D:2 * D])
    f = lb + (1.0 - lb) * jax.nn.sigmoid(fz)
    lf_s[...] = jnp.log2(jnp.maximum(f, GATE_EPS))
    k_s[...] = 1.0 - f
    v_s[...] = _dot(xb, win_ref[:, 2 * D:3 * D]).astype(bf16)
    g = _dot(xb, win_ref[:, 3 * D:4 * D])
    gate_s[...] = g * jax.nn.sigmoid(g)

    sub = lax.broadcasted_iota(jnp.int32, (SUBLANES, D), 0)
    m1 = (sub % 4) >= 1
    m2 = (sub % 4) >= 2
    lo = sub < 4
    cat = lambda xs: jnp.concatenate(xs, axis=0)
    n_chunks = x.shape[0] // CHUNK

    for c in range(n_chunks):
        rows = slice(c * CHUNK, (c + 1) * CHUNK)
        lf = lf_s[rows, :]
        p4, t4, p8, t8 = [], [], [], []
        for r in range(CHUNK // SUBLANES):
            xr = lf[SUBLANES * r:SUBLANES * (r + 1), :]
            s1 = xr + jnp.where(m1, pltpu.roll(xr, 1, 0), 0.0)
            a4 = s1 + jnp.where(m2, pltpu.roll(s1, 2, 0), 0.0)
            b4 = jnp.broadcast_to(a4[3:4, :], (SUBLANES, D))
            a8 = a4 + jnp.where(lo, 0.0, b4)
            b8 = jnp.broadcast_to(a8[7:8, :], (SUBLANES, D))
            p4.append(a4); t4.append(b4); p8.append(a8); t8.append(b8)
        t16 = [t8[0] + t8[1], t8[2] + t8[3]]
        t32 = t16[0] + t16[1]
        p16 = [p8[0], p8[1] + t8[0], p8[2], p8[3] + t8[2]]
        p32 = [p16[0], p16[1], p16[2] + t16[0], p16[3] + t16[0]]

        pl4 = cat(p4)
        e3 = jnp.exp2(cat([jnp.where(lo, t4[r] - p4[r], p4[r]) for r in range(4)]))
        e2 = jnp.exp2(cat([t8[0] - p8[0], p8[1], t8[2] - p8[2], p8[3]]))
        e1 = jnp.exp2(cat([t16[0] - p16[0], t16[0] - p16[1], p16[2], p16[3]]))
        q = q_s[rows, :]
        k = k_s[rows, :]
        for l, ee in enumerate((jnp.exp2(pl4), e3, e2, e1, jnp.exp2(cat(p32)))):
            qs_s[l, rows, :] = (q * ee).astype(bf16)
        for l, ee in enumerate((jnp.exp2(-pl4), e3, e2, e1, jnp.exp2(cat([t32 - p32[r] for r in range(4)])))):
            ks_s[l, rows, :] = (k * ee).astype(bf16)
        dec_s[c] = jnp.exp2(t32)

    mask_leaf, mask3, mask2, mask1 = _chunk_masks()
    heads = [slice(h * HEAD_DIM, (h + 1) * HEAD_DIM) for h in range(HEADS)]

    for c in range(n_chunks):
        rows = slice(c * CHUNK, (c + 1) * CHUNK)
        sc = [[_dot_nt(qs_s[l, rows, hs], ks_s[l, rows, hs]) for l in range(4)] for hs in heads]
        sts = [st_s[h] for h in range(HEADS)]
        o_inter = [_dot_nt(qs_s[4, rows, hs], sts[h].astype(bf16)) for h, hs in enumerate(heads)]
        upd = [_dot_tn(v_s[rows, hs], ks_s[4, rows, hs]) for hs in heads]
        for h, hs in enumerate(heads):
            st_s[h] = sts[h] * dec_s[c, 0:1, hs] + upd[h]
        ps = [jnp.where(mask_leaf, s[0], jnp.where(mask3, s[1], jnp.where(mask2, s[2],
                        jnp.where(mask1, s[3], 0.0)))).astype(bf16) for s in sc]
        for h, hs in enumerate(heads):
            o_s[rows, hs] = _dot(ps[h], v_s[rows, hs]) + o_inter[h]

    for h, hs in enumerate(heads):
        o = o_s[:, hs]
        o = o * lax.rsqrt(jnp.mean(o * o, axis=-1, keepdims=True) + RMS_EPS)
        og_s[:, hs] = (o * ng_ref[:, hs] * gate_s[:, hs]).astype(bf16)

    mix = _dot(og_s[...], wout_ref[...])
    o_ref[...] = _layer_norm(ALPHA * x + mix, lng_ref[...], lnb_ref[...])


def _hgrn_mixer(x, layer_j, w_in, lb_logits, norm_g, w_out, ln_g, ln_b):
    B, S, D = x.shape
    tt = MIX_TILE
    tile = lambda b, t: (b, t, 0)
    full = lambda b, t: (0, 0)
    n_a = lb_logits.shape[0]
    return pl.pallas_call(
        functools.partial(_hgrn_kernel, layer_j),
        out_shape=jax.ShapeDtypeStruct((B, S, D), f32),
        grid=(B, S // tt),
        in_specs=[
            pl.BlockSpec((None, tt, D), tile),
            pl.BlockSpec((D, 4 * D), full),
            pl.BlockSpec((n_a, D), full),
            pl.BlockSpec((1, D), full),
            pl.BlockSpec((D, D), full),
            pl.BlockSpec((1, D), full),
            pl.BlockSpec((1, D), full),
        ],
        out_specs=pl.BlockSpec((None, tt, D), tile),
        scratch_shapes=[
            pltpu.VMEM((tt, D), f32),
            pltpu.VMEM((tt, D), f32),
            pltpu.VMEM((tt, D), bf16),
            pltpu.VMEM((tt, D), f32),
            pltpu.VMEM((tt, D), f32),
            pltpu.VMEM((5, tt, D), bf16),
            pltpu.VMEM((5, tt, D), bf16),
            pltpu.VMEM((tt // CHUNK, SUBLANES, D), f32),
            pltpu.VMEM((tt, D), f32),
            pltpu.VMEM((tt, D), bf16),
            pltpu.VMEM((HEADS, HEAD_DIM, HEAD_DIM), f32),
        ],
        compiler_params=pltpu.CompilerParams(
            dimension_semantics=("arbitrary", "arbitrary"), vmem_limit_bytes=VMEM_LIMIT_BYTES),
        name="hgrn_mixer",
    )(x, w_in, lb_logits, norm_g, w_out, ln_g, ln_b)


def _conv_kernel(x_ref, w1_ref, b1_ref, wdw_ref, bdw_ref, cg_ref, cb_ref, w2_ref, b2_ref,
                 lng_ref, lnb_ref, o_ref, u_s, c_s):
    D = D_MODEL
    tt = x_ref.shape[0]
    n_slabs = D // LANES

    @pl.when(pl.program_id(1) == 0)
    def _():
        u_s[:, 0:CONV_HIST, :] = jnp.zeros((n_slabs, CONV_HIST, LANES), f32)

    x = x_ref[...]
    xb = x.astype(bf16)
    a = _dot(xb, w1_ref[:, 0:D]) + b1_ref[:, 0:D]
    gate = _dot(xb, w1_ref[:, D:2 * D]) + b1_ref[:, D:2 * D]
    glu = a * jax.nn.sigmoid(gate)
    for s in range(n_slabs):
        u_s[s, CONV_HIST:CONV_HIST + tt, :] = glu[:, s * LANES:(s + 1) * LANES]

    base = CONV_HIST - (CONV_WIDTH - 1)
    for s in range(n_slabs):
        cols = slice(s * LANES, (s + 1) * LANES)
        for rb in range(tt // CONV_ROWS):
            acc = jnp.zeros((CONV_ROWS, LANES), f32)
            for j in range(CONV_WIDTH):
                r = rb * CONV_ROWS + base + j
                acc = acc + u_s[s, r:r + CONV_ROWS, :] * wdw_ref[j:j + 1, cols]
            c_s[rb * CONV_ROWS:(rb + 1) * CONV_ROWS, cols] = acc + bdw_ref[:, cols]

    u_s[:, 0:CONV_HIST, :] = u_s[:, tt:tt + CONV_HIST, :]

    u = _layer_norm(c_s[...], cg_ref[...], cb_ref[...])
    u = u * jax.nn.sigmoid(u)
    mix = _dot(u.astype(bf16), w2_ref[...]) + b2_ref[...]
    o_ref[...] = _layer_norm(ALPHA * x + mix, lng_ref[...], lnb_ref[...])


def _conv_mixer(x, w1, b1, w_dw, b_dw, cg, cb, w2, b2, ln_g, ln_b):
    B, S, D = x.shape
    tt = MIX_TILE
    tile = lambda b, t: (b, t, 0)
    full = lambda b, t: (0, 0)
    vec = pl.BlockSpec((1, D), full)
    return pl.pallas_call(
        _conv_kernel,
        out_shape=jax.ShapeDtypeStruct((B, S, D), f32),
        grid=(B, S // tt),
        in_specs=[
            pl.BlockSpec((None, tt, D), tile),
            pl.BlockSpec((D, 2 * D), full),
            pl.BlockSpec((1, 2 * D), full),
            pl.BlockSpec((CONV_WIDTH, D), full),
            vec, vec, vec,
            pl.BlockSpec((D, D), full),
            vec, vec, vec,
        ],
        out_specs=pl.BlockSpec((None, tt, D), tile),
        scratch_shapes=[
            pltpu.VMEM((D // LANES, CONV_HIST + tt, LANES), f32),
            pltpu.VMEM((tt, D), f32),
        ],
        compiler_params=pltpu.CompilerParams(
            dimension_semantics=("arbitrary", "arbitrary"), vmem_limit_bytes=VMEM_LIMIT_BYTES),
        name="conv_mixer",
    )(x, w1, b1, w_dw, b_dw, cg, cb, w2, b2, ln_g, ln_b)


def kernel(x, ln_mix_g, ln_mix_b, ln_ffn_g, ln_ffn_b, ffn_w1, ffn_w2, a_w_in, a_lb_logits, a_norm_g, a_w_out, b_w_pw1, b_b_pw1, b_w_dw, b_b_dw, b_ln_g, b_ln_b, b_w_pw2, b_b_pw2):
    B, S, D = x.shape
    assert D == D_MODEL and S % MIX_TILE == 0 and (B * S) % FFN_TILE == 0
    row = lambda a: a.reshape(1, -1)
    lb_logits = a_lb_logits.astype(f32)
    ffn_w1, ffn_w2, a_w_in, a_w_out, b_w_pw1, b_w_pw2 = (
        w.astype(bf16) for w in (ffn_w1, ffn_w2, a_w_in, a_w_out, b_w_pw1, b_w_pw2))
    for i in range(DEPTH):
        j = i // N_MIXERS
        if i % N_MIXERS == 0:
            x = _hgrn_mixer(x, j, a_w_in[j], lb_logits, row(a_norm_g[j]),
                            a_w_out[j], row(ln_mix_g[i]), row(ln_mix_b[i]))
        else:
            x = _conv_mixer(x, b_w_pw1[j], row(b_b_pw1[j]), b_w_dw[j], row(b_b_dw[j]),
                            row(b_ln_g[j]), row(b_ln_b[j]), b_w_pw2[j], row(b_b_pw2[j]),
                            row(ln_mix_g[i]), row(ln_mix_b[i]))
        x = _ffn(x.reshape(B * S, D), ffn_w1[i], ffn_w2[i],
                 row(ln_ffn_g[i]), row(ln_ffn_b[i])).reshape(B, S, D)
    return x
```

```python
import functools

import jax
import jax.numpy as jnp
from jax import lax
from jax.experimental import pallas as pl
from jax.experimental.pallas import tpu as pltpu

D_MODEL = 1024
DEPTH = 4
N_MIXERS = 2
HEADS = 8
HEAD_DIM = 128
CHUNK = 32
CONV_WIDTH = 31
D_FF = 4 * D_MODEL
ALPHA = (2.0 * DEPTH) ** 0.25
LN_EPS = 1e-5
RMS_EPS = 1e-6
GATE_EPS = 1e-6

SUBLANES = 8
LANES = 128
VMEM_LIMIT_BYTES = 56 * 1024 * 1024

MIX_TILE = 512
FFN_TILE = 1024
FFN_CHUNK = 1024
CONV_HIST = 32
CONV_ROWS = 64

bf16 = jnp.bfloat16
f32 = jnp.float32


def _layer_norm(y, g, b):
    mu = jnp.mean(y, axis=-1, keepdims=True)
    yc = y - mu
    var = jnp.mean(yc * yc, axis=-1, keepdims=True)
    return yc * lax.rsqrt(var + LN_EPS) * g + b


def _dot(a, b):
    return jnp.dot(a, b, preferred_element_type=f32)


def _dot_nt(a, b):
    return lax.dot_general(a, b, (((1,), (1,)), ((), ())), preferred_element_type=f32)


def _dot_tn(a, b):
    return lax.dot_general(a, b, (((0,), (0,)), ((), ())), preferred_element_type=f32)


def _ffn_kernel(x_ref, w1_ref, w2_ref, g_ref, b_ref, o_ref):
    x = x_ref[...]
    xb = x.astype(bf16)
    acc = None
    for c in range(D_FF // FFN_CHUNK):
        cols = slice(c * FFN_CHUNK, (c + 1) * FFN_CHUNK)
        h = _dot(xb, w1_ref[:, cols])
        h = jnp.square(jnp.maximum(h, 0.0)).astype(bf16)
        part = _dot(h, w2_ref[cols, :])
        acc = part if acc is None else acc + part
    o_ref[...] = _layer_norm(ALPHA * x + acc, g_ref[...], b_ref[...])


def _layer_weight(layer, rows, cols, n_grid_axes):
    index = (lambda i: (layer, 0, 0)) if n_grid_axes == 1 else (lambda b, t: (layer, 0, 0))
    return pl.BlockSpec((None, rows, cols), index, pipeline_mode=pl.Buffered(1))


def _ffn(x2d, layer, w1, w2, g, b):
    n_tok = x2d.shape[0]
    row = lambda i: (i, 0)
    full = lambda i: (0, 0)
    return pl.pallas_call(
        _ffn_kernel,
        out_shape=jax.ShapeDtypeStruct((n_tok, D_MODEL), f32),
        grid=(n_tok // FFN_TILE,),
        in_specs=[
            pl.BlockSpec((FFN_TILE, D_MODEL), row),
            _layer_weight(layer, D_MODEL, D_FF, 1),
            _layer_weight(layer, D_FF, D_MODEL, 1),
            pl.BlockSpec((1, D_MODEL), full),
            pl.BlockSpec((1, D_MODEL), full),
        ],
        out_specs=pl.BlockSpec((FFN_TILE, D_MODEL), row),
        compiler_params=pltpu.CompilerParams(
            dimension_semantics=("arbitrary",), vmem_limit_bytes=VMEM_LIMIT_BYTES),
        name="ffn",
    )(x2d, w1, w2, g, b)


def _chunk_masks():
    t = lax.broadcasted_iota(jnp.int32, (CHUNK, CHUNK), 0)
    s = lax.broadcasted_iota(jnp.int32, (CHUNK, CHUNK), 1)
    leaf = (t // 4 == s // 4) & (s <= t)
    lvl3 = (t // 8 == s // 8) & (t % 8 >= 4) & (s % 8 < 4)
    lvl2 = (t // 16 == s // 16) & (t % 16 >= 8) & (s % 16 < 8)
    lvl1 = (t >= 16) & (s < 16)
    return leaf, lvl3, lvl2, lvl1


def _hgrn_kernel(layer_j, x_ref, win_ref, lbl_ref, ng_ref, wout_ref, lng_ref, lnb_ref, o_ref,
                 q_s, k_s, v_s, lf_s, gate_s, qs_s, ks_s, dec_s, o_s, og_s, st_s):
    D = D_MODEL

    @pl.when(pl.program_id(1) == 0)
    def _():
        st_s[...] = jnp.zeros_like(st_s)

    x = x_ref[...]
    xb = x.astype(bf16)

    logits = lbl_ref[...]
    e = jnp.exp(logits - jnp.max(logits, axis=0, keepdims=True))
    soft = e / jnp.sum(e, axis=0, keepdims=True)
    lb = jnp.zeros((1, D), f32)
    for l in range(1, layer_j + 1):
        lb = lb + soft[l:l + 1, :]

    q = _dot(xb, win_ref[:, 0:D])
    q_s[...] = q * jax.nn.sigmoid(q)
    fz = _dot(xb, win_ref[:, D:2 * D])
    f = lb + (1.0 - lb) * jax.nn.sigmoid(fz)
    lf_s[...] = jnp.log2(jnp.maximum(f, GATE_EPS))
    k_s[...] = 1.0 - f
    v_s[...] = _dot(xb, win_ref[:, 2 * D:3 * D]).astype(bf16)
    g = _dot(xb, win_ref[:, 3 * D:4 * D])
    gate_s[...] = g * jax.nn.sigmoid(g)

    sub = lax.broadcasted_iota(jnp.int32, (SUBLANES, D), 0)
    m1 = (sub % 4) >= 1
    m2 = (sub % 4) >= 2
    lo = sub < 4
    cat = lambda xs: jnp.concatenate(xs, axis=0)
    n_chunks = x.shape[0] // CHUNK

    for c in range(n_chunks):
        rows = slice(c * CHUNK, (c + 1) * CHUNK)
        lf = lf_s[rows, :]
        p4, t4, p8, t8 = [], [], [], []
        for r in range(CHUNK // SUBLANES):
            xr = lf[SUBLANES * r:SUBLANES * (r + 1), :]
            s1 = xr + jnp.where(m1, pltpu.roll(xr, 1, 0), 0.0)
            a4 = s1 + jnp.where(m2, pltpu.roll(s1, 2, 0), 0.0)
            b4 = jnp.broadcast_to(a4[3:4, :], (SUBLANES, D))
            a8 = a4 + jnp.where(lo, 0.0, b4)
            b8 = jnp.broadcast_to(a8[7:8, :], (SUBLANES, D))
            p4.append(a4); t4.append(b4); p8.append(a8); t8.append(b8)
        t16 = [t8[0] + t8[1], t8[2] + t8[3]]
        t32 = t16[0] + t16[1]
        p16 = [p8[0], p8[1] + t8[0], p8[2], p8[3] + t8[2]]
        p32 = [p16[0], p16[1], p16[2] + t16[0], p16[3] + t16[0]]

        pl4 = cat(p4)
        e3 = jnp.exp2(cat([jnp.where(lo, t4[r] - p4[r], p4[r]) for r in range(4)]))
        e2 = jnp.exp2(cat([t8[0] - p8[0], p8[1], t8[2] - p8[2], p8[3]]))
        e1 = jnp.exp2(cat([t16[0] - p16[0], t16[0] - p16[1], p16[2], p16[3]]))
        q = q_s[rows, :]
        k = k_s[rows, :]
        for l, ee in enumerate((jnp.exp2(pl4), e3, e2, e1, jnp.exp2(cat(p32)))):
            qs_s[l, rows, :] = (q * ee).astype(bf16)
        for l, ee in enumerate((jnp.exp2(-pl4), e3, e2, e1, jnp.exp2(cat([t32 - p32[r] for r in range(4)])))):
            ks_s[l, rows, :] = (k * ee).astype(bf16)
        dec_s[c] = jnp.exp2(t32)

    mask_leaf, mask3, mask2, mask1 = _chunk_masks()
    heads = [slice(h * HEAD_DIM, (h + 1) * HEAD_DIM) for h in range(HEADS)]

    for c in range(n_chunks):
        rows = slice(c * CHUNK, (c + 1) * CHUNK)
        sc = [[_dot_nt(qs_s[l, rows, hs], ks_s[l, rows, hs]) for l in range(4)] for hs in heads]
        sts = [st_s[h] for h in range(HEADS)]
        o_inter = [_dot_nt(qs_s[4, rows, hs], sts[h].astype(bf16)) for h, hs in enumerate(heads)]
        upd = [_dot_tn(v_s[rows, hs], ks_s[4, rows, hs]) for hs in heads]
        for h, hs in enumerate(heads):
            st_s[h] = sts[h] * dec_s[c, 0:1, hs] + upd[h]
        ps = [jnp.where(mask_leaf, s[0], jnp.where(mask3, s[1], jnp.where(mask2, s[2],
                        jnp.where(mask1, s[3], 0.0)))).astype(bf16) for s in sc]
        for h, hs in enumerate(heads):
            o_s[rows, hs] = _dot(ps[h], v_s[rows, hs]) + o_inter[h]

    for h, hs in enumerate(heads):
        o = o_s[:, hs]
        o = o * lax.rsqrt(jnp.mean(o * o, axis=-1, keepdims=True) + RMS_EPS)
        og_s[:, hs] = (o * ng_ref[:, hs] * gate_s[:, hs]).astype(bf16)

    mix = _dot(og_s[...], wout_ref[...])
    o_ref[...] = _layer_norm(ALPHA * x + mix, lng_ref[...], lnb_ref[...])


def _hgrn_mixer(x, layer_j, w_in, lb_logits, norm_g, w_out, ln_g, ln_b):
    B, S, D = x.shape
    tt = MIX_TILE
    tile = lambda b, t: (b, t, 0)
    full = lambda b, t: (0, 0)
    n_a = lb_logits.shape[0]
    return pl.pallas_call(
        functools.partial(_hgrn_kernel, layer_j),
        out_shape=jax.ShapeDtypeStruct((B, S, D), f32),
        grid=(B, S // tt),
        in_specs=[
            pl.BlockSpec((None, tt, D), tile),
            _layer_weight(layer_j, D, 4 * D, 2),
            pl.BlockSpec((n_a, D), full),
            pl.BlockSpec((1, D), full),
            _layer_weight(layer_j, D, D, 2),
            pl.BlockSpec((1, D), full),
            pl.BlockSpec((1, D), full),
        ],
        out_specs=pl.BlockSpec((None, tt, D), tile),
        scratch_shapes=[
            pltpu.VMEM((tt, D), f32),
            pltpu.VMEM((tt, D), f32),
            pltpu.VMEM((tt, D), bf16),
            pltpu.VMEM((tt, D), f32),
            pltpu.VMEM((tt, D), f32),
            pltpu.VMEM((5, tt, D), bf16),
            pltpu.VMEM((5, tt, D), bf16),
            pltpu.VMEM((tt // CHUNK, SUBLANES, D), f32),
            pltpu.VMEM((tt, D), f32),
            pltpu.VMEM((tt, D), bf16),
            pltpu.VMEM((HEADS, HEAD_DIM, HEAD_DIM), f32),
        ],
        compiler_params=pltpu.CompilerParams(
            dimension_semantics=("arbitrary", "arbitrary"), vmem_limit_bytes=VMEM_LIMIT_BYTES),
        name="hgrn_mixer",
    )(x, w_in, lb_logits, norm_g, w_out, ln_g, ln_b)


def _conv_kernel(x_ref, w1_ref, b1_ref, wdw_ref, bdw_ref, cg_ref, cb_ref, w2_ref, b2_ref,
                 lng_ref, lnb_ref, o_ref, u_s, c_s):
    D = D_MODEL
    tt = x_ref.shape[0]
    n_slabs = D // LANES

    @pl.when(pl.program_id(1) == 0)
    def _():
        u_s[:, 0:CONV_HIST, :] = jnp.zeros((n_slabs, CONV_HIST, LANES), f32)

    x = x_ref[...]
    xb = x.astype(bf16)
    a = _dot(xb, w1_ref[:, 0:D]) + b1_ref[:, 0:D]
    gate = _dot(xb, w1_ref[:, D:2 * D]) + b1_ref[:, D:2 * D]
    glu = a * jax.nn.sigmoid(gate)
    for s in range(n_slabs):
        u_s[s, CONV_HIST:CONV_HIST + tt, :] = glu[:, s * LANES:(s + 1) * LANES]

    base = CONV_HIST - (CONV_WIDTH - 1)
    for s in range(n_slabs):
        cols = slice(s * LANES, (s + 1) * LANES)
        for rb in range(tt // CONV_ROWS):
            acc = jnp.zeros((CONV_ROWS, LANES), f32)
            for j in range(CONV_WIDTH):
                r = rb * CONV_ROWS + base + j
                acc = acc + u_s[s, r:r + CONV_ROWS, :] * wdw_ref[j:j + 1, cols]
            c_s[rb * CONV_ROWS:(rb + 1) * CONV_ROWS, cols] = acc + bdw_ref[:, cols]

    u_s[:, 0:CONV_HIST, :] = u_s[:, tt:tt + CONV_HIST, :]

    u = _layer_norm(c_s[...], cg_ref[...], cb_ref[...])
    u = u * jax.nn.sigmoid(u)
    mix = _dot(u.astype(bf16), w2_ref[...]) + b2_ref[...]
    o_ref[...] = _layer_norm(ALPHA * x + mix, lng_ref[...], lnb_ref[...])


def _conv_mixer(x, layer_j, w1, b1, w_dw, b_dw, cg, cb, w2, b2, ln_g, ln_b):
    B, S, D = x.shape
    tt = MIX_TILE
    tile = lambda b, t: (b, t, 0)
    full = lambda b, t: (0, 0)
    vec = pl.BlockSpec((1, D), full)
    return pl.pallas_call(
        _conv_kernel,
        out_shape=jax.ShapeDtypeStruct((B, S, D), f32),
        grid=(B, S // tt),
        in_specs=[
            pl.BlockSpec((None, tt, D), tile),
            _layer_weight(layer_j, D, 2 * D, 2),
            pl.BlockSpec((1, 2 * D), full),
            pl.BlockSpec((CONV_WIDTH, D), full),
            vec, vec, vec,
            _layer_weight(layer_j, D, D, 2),
            vec, vec, vec,
        ],
        out_specs=pl.BlockSpec((None, tt, D), tile),
        scratch_shapes=[
            pltpu.VMEM((D // LANES, CONV_HIST + tt, LANES), f32),
            pltpu.VMEM((tt, D), f32),
        ],
        compiler_params=pltpu.CompilerParams(
            dimension_semantics=("arbitrary", "arbitrary"), vmem_limit_bytes=VMEM_LIMIT_BYTES),
        name="conv_mixer",
    )(x, w1, b1, w_dw, b_dw, cg, cb, w2, b2, ln_g, ln_b)


def kernel(x, ln_mix_g, ln_mix_b, ln_ffn_g, ln_ffn_b, ffn_w1, ffn_w2, a_w_in, a_lb_logits, a_norm_g, a_w_out, b_w_pw1, b_b_pw1, b_w_dw, b_b_dw, b_ln_g, b_ln_b, b_w_pw2, b_b_pw2):
    B, S, D = x.shape
    assert D == D_MODEL and S % MIX_TILE == 0 and (B * S) % FFN_TILE == 0
    row = lambda a: a.reshape(1, -1)
    lb_logits = a_lb_logits.astype(f32)
    ffn_w1, ffn_w2, a_w_in, a_w_out, b_w_pw1, b_w_pw2 = (
        w.astype(bf16) for w in (ffn_w1, ffn_w2, a_w_in, a_w_out, b_w_pw1, b_w_pw2))
    for i in range(DEPTH):
        j = i // N_MIXERS
        if i % N_MIXERS == 0:
            x = _hgrn_mixer(x, j, a_w_in, lb_logits, row(a_norm_g[j]),
                            a_w_out, row(ln_mix_g[i]), row(ln_mix_b[i]))
        else:
            x = _conv_mixer(x, j, b_w_pw1, row(b_b_pw1[j]), b_w_dw[j], row(b_b_dw[j]),
                            row(b_ln_g[j]), row(b_ln_b[j]), b_w_pw2, row(b_b_pw2[j]),
                            row(ln_mix_g[i]), row(ln_mix_b[i]))
        x = _ffn(x.reshape(B * S, D), i, ffn_w1, ffn_w2,
                 row(ln_ffn_g[i]), row(ln_ffn_b[i])).reshape(B, S, D)
    return x
```

```python
import functools

import jax
import jax.numpy as jnp
from jax import lax
from jax.experimental import pallas as pl
from jax.experimental.pallas import tpu as pltpu

D_MODEL = 1024
DEPTH = 4
N_MIXERS = 2
HEADS = 8
HEAD_DIM = 128
CHUNK = 32
CONV_WIDTH = 31
D_FF = 4 * D_MODEL
ALPHA = (2.0 * DEPTH) ** 0.25
LN_EPS = 1e-5
RMS_EPS = 1e-6
GATE_EPS = 1e-6

SUBLANES = 8
LANES = 128
VMEM_LIMIT_BYTES = 56 * 1024 * 1024

MIX_TILE = 512
FFN_TILE = 1024
FFN_CHUNK = 1024
CONV_HIST = 32
CONV_ROWS = 64

bf16 = jnp.bfloat16
f32 = jnp.float32


def _layer_norm(y, g, b):
    mu = jnp.mean(y, axis=-1, keepdims=True)
    yc = y - mu
    var = jnp.mean(yc * yc, axis=-1, keepdims=True)
    return yc * lax.rsqrt(var + LN_EPS) * g + b


def _dot(a, b):
    return jnp.dot(a, b, preferred_element_type=f32)


def _dot_nt(a, b):
    return lax.dot_general(a, b, (((1,), (1,)), ((), ())), preferred_element_type=f32)


def _dot_tn(a, b):
    return lax.dot_general(a, b, (((0,), (0,)), ((), ())), preferred_element_type=f32)


def _ffn_kernel(x_ref, w1_ref, w2_ref, g_ref, b_ref, o_ref):
    x = x_ref[...]
    xb = x.astype(bf16)
    acc = None
    for c in range(D_FF // FFN_CHUNK):
        cols = slice(c * FFN_CHUNK, (c + 1) * FFN_CHUNK)
        h = _dot(xb, w1_ref[:, cols])
        h = jnp.square(jnp.maximum(h, 0.0)).astype(bf16)
        part = _dot(h, w2_ref[cols, :])
        acc = part if acc is None else acc + part
    o_ref[...] = _layer_norm(ALPHA * x + acc, g_ref[...], b_ref[...])


def _layer_weight(layer, rows, cols, n_grid_axes):
    index = (lambda i: (layer, 0, 0)) if n_grid_axes == 1 else (lambda b, t: (layer, 0, 0))
    return pl.BlockSpec((None, rows, cols), index, pipeline_mode=pl.Buffered(1))


def _ffn(x2d, layer, w1, w2, g, b):
    n_tok = x2d.shape[0]
    row = lambda i: (i, 0)
    full = lambda i: (0, 0)
    return pl.pallas_call(
        _ffn_kernel,
        out_shape=jax.ShapeDtypeStruct((n_tok, D_MODEL), f32),
        grid=(n_tok // FFN_TILE,),
        in_specs=[
            pl.BlockSpec((FFN_TILE, D_MODEL), row),
            _layer_weight(layer, D_MODEL, D_FF, 1),
            _layer_weight(layer, D_FF, D_MODEL, 1),
            pl.BlockSpec((1, D_MODEL), full),
            pl.BlockSpec((1, D_MODEL), full),
        ],
        out_specs=pl.BlockSpec((FFN_TILE, D_MODEL), row),
        compiler_params=pltpu.CompilerParams(
            dimension_semantics=("arbitrary",), vmem_limit_bytes=VMEM_LIMIT_BYTES),
        name="ffn",
    )(x2d, w1, w2, g, b)


def _chunk_masks():
    t = lax.broadcasted_iota(jnp.int32, (CHUNK, CHUNK), 0)
    s = lax.broadcasted_iota(jnp.int32, (CHUNK, CHUNK), 1)
    leaf = (t // 4 == s // 4) & (s <= t)
    lvl3 = (t // 8 == s // 8) & (t % 8 >= 4) & (s % 8 < 4)
    lvl2 = (t // 16 == s // 16) & (t % 16 >= 8) & (s % 16 < 8)
    lvl1 = (t >= 16) & (s < 16)
    return leaf, lvl3, lvl2, lvl1


def _hgrn_kernel(layer_j, x_ref, win_ref, lbl_ref, ng_ref, wout_ref, lng_ref, lnb_ref, o_ref,
                 q_s, k_s, v_s, lf_s, gate_s, qs_s, ks_s, dec_s, o_s, og_s, st_s):
    D = D_MODEL

    @pl.when(pl.program_id(1) == 0)
    def _():
        st_s[...] = jnp.zeros_like(st_s)

    x = x_ref[...]
    xb = x.astype(bf16)

    logits = lbl_ref[...]
    e = jnp.exp(logits - jnp.max(logits, axis=0, keepdims=True))
    soft = e / jnp.sum(e, axis=0, keepdims=True)
    lb = jnp.zeros((1, D), f32)
    for l in range(1, layer_j + 1):
        lb = lb + soft[l:l + 1, :]

    fz = _dot(xb, win_ref[:, D:2 * D])
    f = lb + (1.0 - lb) * jax.nn.sigmoid(fz)
    lf_s[...] = jnp.log2(jnp.maximum(f, GATE_EPS))
    k_s[...] = 1.0 - f
    q = _dot(xb, win_ref[:, 0:D])
    q_s[...] = q * jax.nn.sigmoid(q)
    v_s[...] = _dot(xb, win_ref[:, 2 * D:3 * D]).astype(bf16)
    g = _dot(xb, win_ref[:, 3 * D:4 * D])
    gate_s[...] = g * jax.nn.sigmoid(g)

    sub = lax.broadcasted_iota(jnp.int32, (SUBLANES, D), 0)
    m1 = (sub % 4) >= 1
    m2 = (sub % 4) >= 2
    lo = sub < 4
    cat = lambda xs: jnp.concatenate(xs, axis=0)
    n_chunks = x.shape[0] // CHUNK

    for c in range(n_chunks):
        rows = slice(c * CHUNK, (c + 1) * CHUNK)
        lf = lf_s[rows, :]
        p4, t4, p8, t8 = [], [], [], []
        for r in range(CHUNK // SUBLANES):
            xr = lf[SUBLANES * r:SUBLANES * (r + 1), :]
            s1 = xr + jnp.where(m1, pltpu.roll(xr, 1, 0), 0.0)
            a4 = s1 + jnp.where(m2, pltpu.roll(s1, 2, 0), 0.0)
            b4 = jnp.broadcast_to(a4[3:4, :], (SUBLANES, D))
            a8 = a4 + jnp.where(lo, 0.0, b4)
            b8 = jnp.broadcast_to(a8[7:8, :], (SUBLANES, D))
            p4.append(a4); t4.append(b4); p8.append(a8); t8.append(b8)
        t16 = [t8[0] + t8[1], t8[2] + t8[3]]
        t32 = t16[0] + t16[1]
        p16 = [p8[0], p8[1] + t8[0], p8[2], p8[3] + t8[2]]
        p32 = [p16[0], p16[1], p16[2] + t16[0], p16[3] + t16[0]]

        pl4 = cat(p4)
        e3 = jnp.exp2(cat([jnp.where(lo, t4[r] - p4[r], p4[r]) for r in range(4)]))
        e2 = jnp.exp2(cat([t8[0] - p8[0], p8[1], t8[2] - p8[2], p8[3]]))
        e1 = jnp.exp2(cat([t16[0] - p16[0], t16[0] - p16[1], p16[2], p16[3]]))
        q = q_s[rows, :]
        k = k_s[rows, :]
        for l, ee in enumerate((jnp.exp2(pl4), e3, e2, e1, jnp.exp2(cat(p32)))):
            qs_s[l, rows, :] = (q * ee).astype(bf16)
        for l, ee in enumerate((jnp.exp2(-pl4), e3, e2, e1, jnp.exp2(cat([t32 - p32[r] for r in range(4)])))):
            ks_s[l, rows, :] = (k * ee).astype(bf16)
        dec_s[c] = jnp.exp2(t32)

    mask_leaf, mask3, mask2, mask1 = _chunk_masks()
    heads = [slice(h * HEAD_DIM, (h + 1) * HEAD_DIM) for h in range(HEADS)]

    for c in range(n_chunks):
        rows = slice(c * CHUNK, (c + 1) * CHUNK)
        sc = [[_dot_nt(qs_s[l, rows, hs], ks_s[l, rows, hs]) for l in range(4)] for hs in heads]
        sts = [st_s[h] for h in range(HEADS)]
        o_inter = [_dot(qs_s[4, rows, hs], sts[h].astype(bf16)) for h, hs in enumerate(heads)]
        upd = [_dot_tn(ks_s[4, rows, hs], v_s[rows, hs]) for hs in heads]
        dec = jnp.concatenate([dec_s[c]] * (HEAD_DIM // SUBLANES), axis=0)
        for h, hs in enumerate(heads):
            st_s[h] = sts[h] * dec[:, hs].T + upd[h]
        ps = [jnp.where(mask_leaf, s[0], jnp.where(mask3, s[1], jnp.where(mask2, s[2],
                        jnp.where(mask1, s[3], 0.0)))).astype(bf16) for s in sc]
        for h, hs in enumerate(heads):
            o_s[rows, hs] = _dot(ps[h], v_s[rows, hs]) + o_inter[h]

    for h, hs in enumerate(heads):
        o = o_s[:, hs]
        o = o * lax.rsqrt(jnp.mean(o * o, axis=-1, keepdims=True) + RMS_EPS)
        og_s[:, hs] = (o * ng_ref[:, hs] * gate_s[:, hs]).astype(bf16)

    mix = _dot(og_s[...], wout_ref[...])
    o_ref[...] = _layer_norm(ALPHA * x + mix, lng_ref[...], lnb_ref[...])


def _hgrn_mixer(x, layer_j, w_in, lb_logits, norm_g, w_out, ln_g, ln_b):
    B, S, D = x.shape
    tt = MIX_TILE
    tile = lambda b, t: (b, t, 0)
    full = lambda b, t: (0, 0)
    n_a = lb_logits.shape[0]
    return pl.pallas_call(
        functools.partial(_hgrn_kernel, layer_j),
        out_shape=jax.ShapeDtypeStruct((B, S, D), f32),
        grid=(B, S // tt),
        in_specs=[
            pl.BlockSpec((None, tt, D), tile),
            _layer_weight(layer_j, D, 4 * D, 2),
            pl.BlockSpec((n_a, D), full),
            pl.BlockSpec((1, D), full),
            _layer_weight(layer_j, D, D, 2),
            pl.BlockSpec((1, D), full),
            pl.BlockSpec((1, D), full),
        ],
        out_specs=pl.BlockSpec((None, tt, D), tile),
        scratch_shapes=[
            pltpu.VMEM((tt, D), f32),
            pltpu.VMEM((tt, D), f32),
            pltpu.VMEM((tt, D), bf16),
            pltpu.VMEM((tt, D), f32),
            pltpu.VMEM((tt, D), f32),
            pltpu.VMEM((5, tt, D), bf16),
            pltpu.VMEM((5, tt, D), bf16),
            pltpu.VMEM((tt // CHUNK, SUBLANES, D), f32),
            pltpu.VMEM((tt, D), f32),
            pltpu.VMEM((tt, D), bf16),
            pltpu.VMEM((HEADS, HEAD_DIM, HEAD_DIM), f32),
        ],
        compiler_params=pltpu.CompilerParams(
            dimension_semantics=("arbitrary", "arbitrary"), vmem_limit_bytes=VMEM_LIMIT_BYTES),
        name="hgrn_mixer",
    )(x, w_in, lb_logits, norm_g, w_out, ln_g, ln_b)


def _conv_kernel(x_ref, w1_ref, b1_ref, wdw_ref, bdw_ref, cg_ref, cb_ref, w2_ref, b2_ref,
                 lng_ref, lnb_ref, o_ref, u_s, c_s):
    D = D_MODEL
    tt = x_ref.shape[0]
    n_slabs = D // LANES

    @pl.when(pl.program_id(1) == 0)
    def _():
        u_s[:, 0:CONV_HIST, :] = jnp.zeros((n_slabs, CONV_HIST, LANES), f32)

    x = x_ref[...]
    xb = x.astype(bf16)
    a = _dot(xb, w1_ref[:, 0:D]) + b1_ref[:, 0:D]
    gate = _dot(xb, w1_ref[:, D:2 * D]) + b1_ref[:, D:2 * D]
    glu = a * jax.nn.sigmoid(gate)
    for s in range(n_slabs):
        u_s[s, CONV_HIST:CONV_HIST + tt, :] = glu[:, s * LANES:(s + 1) * LANES]

    base = CONV_HIST - (CONV_WIDTH - 1)
    for s in range(n_slabs):
        cols = slice(s * LANES, (s + 1) * LANES)
        for rb in range(tt // CONV_ROWS):
            acc = jnp.zeros((CONV_ROWS, LANES), f32)
            for j in range(CONV_WIDTH):
                r = rb * CONV_ROWS + base + j
                acc = acc + u_s[s, r:r + CONV_ROWS, :] * wdw_ref[j:j + 1, cols]
            c_s[rb * CONV_ROWS:(rb + 1) * CONV_ROWS, cols] = acc + bdw_ref[:, cols]

    u_s[:, 0:CONV_HIST, :] = u_s[:, tt:tt + CONV_HIST, :]

    u = _layer_norm(c_s[...], cg_ref[...], cb_ref[...])
    u = u * jax.nn.sigmoid(u)
    mix = _dot(u.astype(bf16), w2_ref[...]) + b2_ref[...]
    o_ref[...] = _layer_norm(ALPHA * x + mix, lng_ref[...], lnb_ref[...])


def _conv_mixer(x, layer_j, w1, b1, w_dw, b_dw, cg, cb, w2, b2, ln_g, ln_b):
    B, S, D = x.shape
    tt = MIX_TILE
    tile = lambda b, t: (b, t, 0)
    full = lambda b, t: (0, 0)
    vec = pl.BlockSpec((1, D), full)
    return pl.pallas_call(
        _conv_kernel,
        out_shape=jax.ShapeDtypeStruct((B, S, D), f32),
        grid=(B, S // tt),
        in_specs=[
            pl.BlockSpec((None, tt, D), tile),
            _layer_weight(layer_j, D, 2 * D, 2),
            pl.BlockSpec((1, 2 * D), full),
            pl.BlockSpec((CONV_WIDTH, D), full),
            vec, vec, vec,
            _layer_weight(layer_j, D, D, 2),
            vec, vec, vec,
        ],
        out_specs=pl.BlockSpec((None, tt, D), tile),
        scratch_shapes=[
            pltpu.VMEM((D // LANES, CONV_HIST + tt, LANES), f32),
            pltpu.VMEM((tt, D), f32),
        ],
        compiler_params=pltpu.CompilerParams(
            dimension_semantics=("arbitrary", "arbitrary"), vmem_limit_bytes=VMEM_LIMIT_BYTES),
        name="conv_mixer",
    )(x, w1, b1, w_dw, b_dw, cg, cb, w2, b2, ln_g, ln_b)


def kernel(x, ln_mix_g, ln_mix_b, ln_ffn_g, ln_ffn_b, ffn_w1, ffn_w2, a_w_in, a_lb_logits, a_norm_g, a_w_out, b_w_pw1, b_b_pw1, b_w_dw, b_b_dw, b_ln_g, b_ln_b, b_w_pw2, b_b_pw2):
    B, S, D = x.shape
    assert D == D_MODEL and S % MIX_TILE == 0 and (B * S) % FFN_TILE == 0
    row = lambda a: a.reshape(1, -1)
    lb_logits = a_lb_logits.astype(f32)
    ffn_w1, ffn_w2, a_w_in, a_w_out, b_w_pw1, b_w_pw2 = (
        w.astype(bf16) for w in (ffn_w1, ffn_w2, a_w_in, a_w_out, b_w_pw1, b_w_pw2))
    for i in range(DEPTH):
        j = i // N_MIXERS
        if i % N_MIXERS == 0:
            x = _hgrn_mixer(x, j, a_w_in, lb_logits, row(a_norm_g[j]),
                            a_w_out, row(ln_mix_g[i]), row(ln_mix_b[i]))
        else:
            x = _conv_mixer(x, j, b_w_pw1, row(b_b_pw1[j]), b_w_dw[j], row(b_b_dw[j]),
                            row(b_ln_g[j]), row(b_ln_b[j]), b_w_pw2, row(b_b_pw2[j]),
                            row(ln_mix_g[i]), row(ln_mix_b[i]))
        x = _ffn(x.reshape(B * S, D), i, ffn_w1, ffn_w2,
                 row(ln_ffn_g[i]), row(ln_ffn_b[i])).reshape(B, S, D)
    return x
```

```python
import functools

import jax
import jax.numpy as jnp
from jax import lax
from jax.experimental import pallas as pl
from jax.experimental.pallas import tpu as pltpu

D_MODEL = 1024
DEPTH = 4
N_MIXERS = 2
HEADS = 8
HEAD_DIM = 128
CHUNK = 32
CONV_WIDTH = 31
D_FF = 4 * D_MODEL
ALPHA = (2.0 * DEPTH) ** 0.25
LN_EPS = 1e-5
RMS_EPS = 1e-6
GATE_EPS = 1e-6

SUBLANES = 8
LANES = 128
VMEM_LIMIT_BYTES = 56 * 1024 * 1024

MIX_TILE = 512
FFN_TILE = 1024
FFN_CHUNK = 1024
CONV_HIST = 32
CONV_ROWS = 64

bf16 = jnp.bfloat16
f32 = jnp.float32


def _layer_norm(y, g, b):
    mu = jnp.mean(y, axis=-1, keepdims=True)
    yc = y - mu
    var = jnp.mean(yc * yc, axis=-1, keepdims=True)
    return yc * lax.rsqrt(var + LN_EPS) * g + b


def _dot(a, b):
    return jnp.dot(a, b, preferred_element_type=f32)


def _dot_nt(a, b):
    return lax.dot_general(a, b, (((1,), (1,)), ((), ())), preferred_element_type=f32)


def _dot_tn(a, b):
    return lax.dot_general(a, b, (((0,), (0,)), ((), ())), preferred_element_type=f32)


def _ffn_kernel(x_ref, w1_ref, w2_ref, g_ref, b_ref, o_ref):
    x = x_ref[...]
    xb = x.astype(bf16)
    acc = None
    for c in range(D_FF // FFN_CHUNK):
        cols = slice(c * FFN_CHUNK, (c + 1) * FFN_CHUNK)
        h = _dot(xb, w1_ref[:, cols])
        h = jnp.square(jnp.maximum(h, 0.0)).astype(bf16)
        part = _dot(h, w2_ref[cols, :])
        acc = part if acc is None else acc + part
    o_ref[...] = _layer_norm(ALPHA * x + acc, g_ref[...], b_ref[...])


def _layer_weight(layer, rows, cols, n_grid_axes):
    index = (lambda i: (layer, 0, 0)) if n_grid_axes == 1 else (lambda b, t: (layer, 0, 0))
    return pl.BlockSpec((None, rows, cols), index, pipeline_mode=pl.Buffered(1))


def _ffn(x2d, layer, w1, w2, g, b):
    n_tok = x2d.shape[0]
    row = lambda i: (i, 0)
    full = lambda i: (0, 0)
    return pl.pallas_call(
        _ffn_kernel,
        out_shape=jax.ShapeDtypeStruct((n_tok, D_MODEL), f32),
        grid=(n_tok // FFN_TILE,),
        in_specs=[
            pl.BlockSpec((FFN_TILE, D_MODEL), row),
            _layer_weight(layer, D_MODEL, D_FF, 1),
            _layer_weight(layer, D_FF, D_MODEL, 1),
            pl.BlockSpec((1, D_MODEL), full),
            pl.BlockSpec((1, D_MODEL), full),
        ],
        out_specs=pl.BlockSpec((FFN_TILE, D_MODEL), row),
        compiler_params=pltpu.CompilerParams(
            dimension_semantics=("arbitrary",), vmem_limit_bytes=VMEM_LIMIT_BYTES),
        name="ffn",
    )(x2d, w1, w2, g, b)


def _chunk_masks():
    t = lax.broadcasted_iota(jnp.int32, (CHUNK, CHUNK), 0)
    s = lax.broadcasted_iota(jnp.int32, (CHUNK, CHUNK), 1)
    leaf = (t // 4 == s // 4) & (s <= t)
    lvl3 = (t // 8 == s // 8) & (t % 8 >= 4) & (s % 8 < 4)
    lvl2 = (t // 16 == s // 16) & (t % 16 >= 8) & (s % 16 < 8)
    lvl1 = (t >= 16) & (s < 16)
    return leaf, lvl3, lvl2, lvl1


def _hgrn_kernel(layer_j, x_ref, win_ref, lbl_ref, ng_ref, wout_ref, lng_ref, lnb_ref, o_ref,
                 q_s, k_s, v_s, lf_s, gate_s, qs_s, ks_s, dec_s, o_s, og_s, st_s):
    D = D_MODEL

    @pl.when(pl.program_id(1) == 0)
    def _():
        st_s[...] = jnp.zeros_like(st_s)

    x = x_ref[...]
    xb = x.astype(bf16)

    logits = lbl_ref[...]
    e = jnp.exp(logits - jnp.max(logits, axis=0, keepdims=True))
    soft = e / jnp.sum(e, axis=0, keepdims=True)
    lb = jnp.zeros((1, D), f32)
    for l in range(1, layer_j + 1):
        lb = lb + soft[l:l + 1, :]

    fz = _dot(xb, win_ref[:, D:2 * D])
    f = lb + (1.0 - lb) * jax.nn.sigmoid(fz)
    lf_s[...] = jnp.log2(jnp.maximum(f, GATE_EPS))
    k_s[...] = 1.0 - f
    q = _dot(xb, win_ref[:, 0:D])
    q_s[...] = q * jax.nn.sigmoid(q)
    v_s[...] = _dot(xb, win_ref[:, 2 * D:3 * D]).astype(bf16)
    g = _dot(xb, win_ref[:, 3 * D:4 * D])
    gate_s[...] = g * jax.nn.sigmoid(g)

    sub = lax.broadcasted_iota(jnp.int32, (SUBLANES, D), 0)
    m1 = (sub % 4) >= 1
    m2 = (sub % 4) >= 2
    lo = sub < 4
    cat = lambda xs: jnp.concatenate(xs, axis=0)
    n_chunks = x.shape[0] // CHUNK

    for c in range(n_chunks):
        rows = slice(c * CHUNK, (c + 1) * CHUNK)
        lf = lf_s[rows, :]
        p4, t4, p8, t8 = [], [], [], []
        for r in range(CHUNK // SUBLANES):
            xr = lf[SUBLANES * r:SUBLANES * (r + 1), :]
            s1 = xr + jnp.where(m1, pltpu.roll(xr, 1, 0), 0.0)
            a4 = s1 + jnp.where(m2, pltpu.roll(s1, 2, 0), 0.0)
            b4 = jnp.broadcast_to(a4[3:4, :], (SUBLANES, D))
            a8 = a4 + jnp.where(lo, 0.0, b4)
            b8 = jnp.broadcast_to(a8[7:8, :], (SUBLANES, D))
            p4.append(a4); t4.append(b4); p8.append(a8); t8.append(b8)
        t16 = [t8[0] + t8[1], t8[2] + t8[3]]
        t32 = t16[0] + t16[1]
        p16 = [p8[0], p8[1] + t8[0], p8[2], p8[3] + t8[2]]
        p32 = [p16[0], p16[1], p16[2] + t16[0], p16[3] + t16[0]]

        pl4 = cat(p4)
        e3 = jnp.exp2(cat([jnp.where(lo, t4[r] - p4[r], p4[r]) for r in range(4)]))
        e2 = jnp.exp2(cat([t8[0] - p8[0], p8[1], t8[2] - p8[2], p8[3]]))
        e1 = jnp.exp2(cat([t16[0] - p16[0], t16[0] - p16[1], p16[2], p16[3]]))
        q = q_s[rows, :]
        k = k_s[rows, :]
        for l, ee in enumerate((jnp.exp2(pl4), e3, e2, e1, jnp.exp2(cat(p32)))):
            qs_s[l, rows, :] = (q * ee).astype(bf16)
        for l, ee in enumerate((jnp.exp2(-pl4), e3, e2, e1, jnp.exp2(cat([t32 - p32[r] for r in range(4)])))):
            ks_s[l, rows, :] = (k * ee).astype(bf16)
        dec_s[c] = jnp.exp2(t32)

    mask_leaf, mask3, mask2, mask1 = _chunk_masks()
    heads = [slice(h * HEAD_DIM, (h + 1) * HEAD_DIM) for h in range(HEADS)]

    for c in range(n_chunks):
        rows = slice(c * CHUNK, (c + 1) * CHUNK)
        sc = [[_dot_nt(qs_s[l, rows, hs], ks_s[l, rows, hs]) for l in range(4)] for hs in heads]
        sts = [st_s[h] for h in range(HEADS)]
        upd = [_dot_tn(ks_s[4, rows, hs], v_s[rows, hs]) for hs in heads]
        dec = jnp.concatenate([dec_s[c]] * (HEAD_DIM // SUBLANES), axis=0)
        for h, hs in enumerate(heads):
            st_s[h] = sts[h] * dec[:, hs].T + upd[h]
        ps = [jnp.where(mask_leaf, s[0], jnp.where(mask3, s[1], jnp.where(mask2, s[2],
                        jnp.where(mask1, s[3], 0.0)))).astype(bf16) for s in sc]
        for h, hs in enumerate(heads):
            lhs = jnp.concatenate([qs_s[4, rows, hs], ps[h]], axis=1)
            rhs = jnp.concatenate([sts[h].astype(bf16), v_s[rows, hs]], axis=0)
            o_s[rows, hs] = _dot(lhs, rhs)

    for h, hs in enumerate(heads):
        o = o_s[:, hs]
        o = o * lax.rsqrt(jnp.mean(o * o, axis=-1, keepdims=True) + RMS_EPS)
        og_s[:, hs] = (o * ng_ref[:, hs] * gate_s[:, hs]).astype(bf16)

    mix = _dot(og_s[...], wout_ref[...])
    o_ref[...] = _layer_norm(ALPHA * x + mix, lng_ref[...], lnb_ref[...])


def _hgrn_mixer(x, layer_j, w_in, lb_logits, norm_g, w_out, ln_g, ln_b):
    B, S, D = x.shape
    tt = MIX_TILE
    tile = lambda b, t: (b, t, 0)
    full = lambda b, t: (0, 0)
    n_a = lb_logits.shape[0]
    return pl.pallas_call(
        functools.partial(_hgrn_kernel, layer_j),
        out_shape=jax.ShapeDtypeStruct((B, S, D), f32),
        grid=(B, S // tt),
        in_specs=[
            pl.BlockSpec((None, tt, D), tile),
            _layer_weight(layer_j, D, 4 * D, 2),
            pl.BlockSpec((n_a, D), full),
            pl.BlockSpec((1, D), full),
            _layer_weight(layer_j, D, D, 2),
            pl.BlockSpec((1, D), full),
            pl.BlockSpec((1, D), full),
        ],
        out_specs=pl.BlockSpec((None, tt, D), tile),
        scratch_shapes=[
            pltpu.VMEM((tt, D), f32),
            pltpu.VMEM((tt, D), f32),
            pltpu.VMEM((tt, D), bf16),
            pltpu.VMEM((tt, D), f32),
            pltpu.VMEM((tt, D), f32),
            pltpu.VMEM((5, tt, D), bf16),
            pltpu.VMEM((5, tt, D), bf16),
            pltpu.VMEM((tt // CHUNK, SUBLANES, D), f32),
            pltpu.VMEM((tt, D), f32),
            pltpu.VMEM((tt, D), bf16),
            pltpu.VMEM((HEADS, HEAD_DIM, HEAD_DIM), f32),
        ],
        compiler_params=pltpu.CompilerParams(
            dimension_semantics=("arbitrary", "arbitrary"), vmem_limit_bytes=VMEM_LIMIT_BYTES),
        name="hgrn_mixer",
    )(x, w_in, lb_logits, norm_g, w_out, ln_g, ln_b)


def _conv_kernel(x_ref, w1_ref, b1_ref, wdw_ref, bdw_ref, cg_ref, cb_ref, w2_ref, b2_ref,
                 lng_ref, lnb_ref, o_ref, u_s, c_s):
    D = D_MODEL
    tt = x_ref.shape[0]
    n_slabs = D // LANES

    @pl.when(pl.program_id(1) == 0)
    def _():
        u_s[:, 0:CONV_HIST, :] = jnp.zeros((n_slabs, CONV_HIST, LANES), f32)

    x = x_ref[...]
    xb = x.astype(bf16)
    a = _dot(xb, w1_ref[:, 0:D]) + b1_ref[:, 0:D]
    gate = _dot(xb, w1_ref[:, D:2 * D]) + b1_ref[:, D:2 * D]
    glu = a * jax.nn.sigmoid(gate)
    for s in range(n_slabs):
        u_s[s, CONV_HIST:CONV_HIST + tt, :] = glu[:, s * LANES:(s + 1) * LANES]

    base = CONV_HIST - (CONV_WIDTH - 1)
    for s in range(n_slabs):
        cols = slice(s * LANES, (s + 1) * LANES)
        for rb in range(tt // CONV_ROWS):
            acc = jnp.zeros((CONV_ROWS, LANES), f32)
            for j in range(CONV_WIDTH):
                r = rb * CONV_ROWS + base + j
                acc = acc + u_s[s, r:r + CONV_ROWS, :] * wdw_ref[j:j + 1, cols]
            c_s[rb * CONV_ROWS:(rb + 1) * CONV_ROWS, cols] = acc + bdw_ref[:, cols]

    u_s[:, 0:CONV_HIST, :] = u_s[:, tt:tt + CONV_HIST, :]

    u = _layer_norm(c_s[...], cg_ref[...], cb_ref[...])
    u = u * jax.nn.sigmoid(u)
    mix = _dot(u.astype(bf16), w2_ref[...]) + b2_ref[...]
    o_ref[...] = _layer_norm(ALPHA * x + mix, lng_ref[...], lnb_ref[...])


def _conv_mixer(x, layer_j, w1, b1, w_dw, b_dw, cg, cb, w2, b2, ln_g, ln_b):
    B, S, D = x.shape
    tt = MIX_TILE
    tile = lambda b, t: (b, t, 0)
    full = lambda b, t: (0, 0)
    vec = pl.BlockSpec((1, D), full)
    return pl.pallas_call(
        _conv_kernel,
        out_shape=jax.ShapeDtypeStruct((B, S, D), f32),
        grid=(B, S // tt),
        in_specs=[
            pl.BlockSpec((None, tt, D), tile),
            _layer_weight(layer_j, D, 2 * D, 2),
            pl.BlockSpec((1, 2 * D), full),
            pl.BlockSpec((CONV_WIDTH, D), full),
            vec, vec, vec,
            _layer_weight(layer_j, D, D, 2),
            vec, vec, vec,
        ],
        out_specs=pl.BlockSpec((None, tt, D), tile),
        scratch_shapes=[
            pltpu.VMEM((D // LANES, CONV_HIST + tt, LANES), f32),
            pltpu.VMEM((tt, D), f32),
        ],
        compiler_params=pltpu.CompilerParams(
            dimension_semantics=("arbitrary", "arbitrary"), vmem_limit_bytes=VMEM_LIMIT_BYTES),
        name="conv_mixer",
    )(x, w1, b1, w_dw, b_dw, cg, cb, w2, b2, ln_g, ln_b)


def kernel(x, ln_mix_g, ln_mix_b, ln_ffn_g, ln_ffn_b, ffn_w1, ffn_w2, a_w_in, a_lb_logits, a_norm_g, a_w_out, b_w_pw1, b_b_pw1, b_w_dw, b_b_dw, b_ln_g, b_ln_b, b_w_pw2, b_b_pw2):
    B, S, D = x.shape
    assert D == D_MODEL and S % MIX_TILE == 0 and (B * S) % FFN_TILE == 0
    row = lambda a: a.reshape(1, -1)
    lb_logits = a_lb_logits.astype(f32)
    ffn_w1, ffn_w2, a_w_in, a_w_out, b_w_pw1, b_w_pw2 = (
        w.astype(bf16) for w in (ffn_w1, ffn_w2, a_w_in, a_w_out, b_w_pw1, b_w_pw2))
    for i in range(DEPTH):
        j = i // N_MIXERS
        if i % N_MIXERS == 0:
            x = _hgrn_mixer(x, j, a_w_in, lb_logits, row(a_norm_g[j]),
                            a_w_out, row(ln_mix_g[i]), row(ln_mix_b[i]))
        else:
            x = _conv_mixer(x, j, b_w_pw1, row(b_b_pw1[j]), b_w_dw[j], row(b_b_dw[j]),
                            row(b_ln_g[j]), row(b_ln_b[j]), b_w_pw2, row(b_b_pw2[j]),
                            row(ln_mix_g[i]), row(ln_mix_b[i]))
        x = _ffn(x.reshape(B * S, D), i, ffn_w1, ffn_w2,
                 row(ln_ffn_g[i]), row(ln_ffn_b[i])).reshape(B, S, D)
    return x
```

```python
import functools

import jax
import jax.numpy as jnp
from jax import lax
from jax.experimental import pallas as pl
from jax.experimental.pallas import tpu as pltpu

D_MODEL = 1024
DEPTH = 4
N_MIXERS = 2
HEADS = 8
HEAD_DIM = 128
CHUNK = 32
CONV_WIDTH = 31
D_FF = 4 * D_MODEL
ALPHA = (2.0 * DEPTH) ** 0.25
LN_EPS = 1e-5
RMS_EPS = 1e-6
GATE_EPS = 1e-6

SUBLANES = 8
LANES = 128
VMEM_LIMIT_BYTES = 56 * 1024 * 1024

MIX_TILE = 512
FFN_TILE = 1024
FFN_CHUNK = 1024
CONV_HIST = 32
CONV_ROWS = 64

bf16 = jnp.bfloat16
f32 = jnp.float32


def _layer_norm(y, g, b):
    mu = jnp.mean(y, axis=-1, keepdims=True)
    yc = y - mu
    var = jnp.mean(yc * yc, axis=-1, keepdims=True)
    return yc * lax.rsqrt(var + LN_EPS) * g + b


def _dot(a, b):
    return jnp.dot(a, b, preferred_element_type=f32)


def _dot_nt(a, b):
    return lax.dot_general(a, b, (((1,), (1,)), ((), ())), preferred_element_type=f32)


def _dot_tn(a, b):
    return lax.dot_general(a, b, (((0,), (0,)), ((), ())), preferred_element_type=f32)


def _ffn_kernel(x_ref, w1_ref, w2_ref, g_ref, b_ref, o_ref):
    x = x_ref[...]
    xb = x.astype(bf16)
    acc = None
    for c in range(D_FF // FFN_CHUNK):
        cols = slice(c * FFN_CHUNK, (c + 1) * FFN_CHUNK)
        h = _dot(xb, w1_ref[:, cols])
        h = jnp.square(jnp.maximum(h, 0.0)).astype(bf16)
        part = _dot(h, w2_ref[cols, :])
        acc = part if acc is None else acc + part
    o_ref[...] = _layer_norm(ALPHA * x + acc, g_ref[...], b_ref[...])


def _layer_weight(layer, rows, cols, n_grid_axes):
    index = (lambda i: (layer, 0, 0)) if n_grid_axes == 1 else (lambda b, t: (layer, 0, 0))
    return pl.BlockSpec((None, rows, cols), index, pipeline_mode=pl.Buffered(1))


def _ffn(x2d, layer, w1, w2, g, b):
    n_tok = x2d.shape[0]
    row = lambda i: (i, 0)
    full = lambda i: (0, 0)
    return pl.pallas_call(
        _ffn_kernel,
        out_shape=jax.ShapeDtypeStruct((n_tok, D_MODEL), f32),
        grid=(n_tok // FFN_TILE,),
        in_specs=[
            pl.BlockSpec((FFN_TILE, D_MODEL), row),
            _layer_weight(layer, D_MODEL, D_FF, 1),
            _layer_weight(layer, D_FF, D_MODEL, 1),
            pl.BlockSpec((1, D_MODEL), full),
            pl.BlockSpec((1, D_MODEL), full),
        ],
        out_specs=pl.BlockSpec((FFN_TILE, D_MODEL), row),
        compiler_params=pltpu.CompilerParams(
            dimension_semantics=("arbitrary",), vmem_limit_bytes=VMEM_LIMIT_BYTES),
        name="ffn",
    )(x2d, w1, w2, g, b)


def _chunk_masks():
    t = lax.broadcasted_iota(jnp.int32, (CHUNK, CHUNK), 0)
    s = lax.broadcasted_iota(jnp.int32, (CHUNK, CHUNK), 1)
    leaf = (t // 4 == s // 4) & (s <= t)
    lvl3 = (t // 8 == s // 8) & (t % 8 >= 4) & (s % 8 < 4)
    lvl2 = (t // 16 == s // 16) & (t % 16 >= 8) & (s % 16 < 8)
    lvl1 = (t >= 16) & (s < 16)
    return leaf, lvl3, lvl2, lvl1


def _hgrn_kernel(layer_j, x_ref, win_ref, lbl_ref, ng_ref, wout_ref, lng_ref, lnb_ref, o_ref,
                 q_s, k_s, v_s, lf_s, gate_s, qs_s, ks_s, dec_s, o_s, og_s, st_s):
    D = D_MODEL

    @pl.when(pl.program_id(1) == 0)
    def _():
        st_s[...] = jnp.zeros_like(st_s)

    x = x_ref[...]
    xb = x.astype(bf16)

    logits = lbl_ref[...]
    e = jnp.exp(logits - jnp.max(logits, axis=0, keepdims=True))
    soft = e / jnp.sum(e, axis=0, keepdims=True)
    lb = jnp.zeros((1, D), f32)
    for l in range(1, layer_j + 1):
        lb = lb + soft[l:l + 1, :]

    fz = _dot(xb, win_ref[:, D:2 * D])
    f = lb + (1.0 - lb) * jax.nn.sigmoid(fz)
    lf_s[...] = jnp.log2(jnp.maximum(f, GATE_EPS))
    k_s[...] = 1.0 - f
    q = _dot(xb, win_ref[:, 0:D])
    q_s[...] = q * jax.nn.sigmoid(q)
    v_s[...] = _dot(xb, win_ref[:, 2 * D:3 * D]).astype(bf16)
    g = _dot(xb, win_ref[:, 3 * D:4 * D])
    gate_s[...] = g * jax.nn.sigmoid(g)

    sub = lax.broadcasted_iota(jnp.int32, (SUBLANES, D), 0)
    m1 = (sub % 4) >= 1
    m2 = (sub % 4) >= 2
    lo = sub < 4
    cat = lambda xs: jnp.concatenate(xs, axis=0)
    n_chunks = x.shape[0] // CHUNK

    for c in range(n_chunks):
        rows = slice(c * CHUNK, (c + 1) * CHUNK)
        lf = lf_s[rows, :]
        p4, t4, p8, t8 = [], [], [], []
        for r in range(CHUNK // SUBLANES):
            xr = lf[SUBLANES * r:SUBLANES * (r + 1), :]
            s1 = xr + jnp.where(m1, pltpu.roll(xr, 1, 0), 0.0)
            a4 = s1 + jnp.where(m2, pltpu.roll(s1, 2, 0), 0.0)
            b4 = jnp.broadcast_to(a4[3:4, :], (SUBLANES, D))
            a8 = a4 + jnp.where(lo, 0.0, b4)
            b8 = jnp.broadcast_to(a8[7:8, :], (SUBLANES, D))
            p4.append(a4); t4.append(b4); p8.append(a8); t8.append(b8)
        t16 = [t8[0] + t8[1], t8[2] + t8[3]]
        t32 = t16[0] + t16[1]
        p16 = [p8[0], p8[1] + t8[0], p8[2], p8[3] + t8[2]]
        p32 = [p16[0], p16[1], p16[2] + t16[0], p16[3] + t16[0]]

        pl4 = cat(p4)
        e3 = jnp.exp2(cat([jnp.where(lo, t4[r] - p4[r], p4[r]) for r in range(4)]))
        e2 = jnp.exp2(cat([t8[0] - p8[0], p8[1], t8[2] - p8[2], p8[3]]))
        e1 = jnp.exp2(cat([t16[0] - p16[0], t16[0] - p16[1], p16[2], p16[3]]))
        q = q_s[rows, :]
        k = k_s[rows, :]
        for l, ee in enumerate((jnp.exp2(pl4), e3, e2, e1, jnp.exp2(cat(p32)))):
            qs_s[l, rows, :] = (q * ee).astype(bf16)
        for l, ee in enumerate((jnp.exp2(-pl4), e3, e2, e1, jnp.exp2(cat([t32 - p32[r] for r in range(4)])))):
            ks_s[l, rows, :] = (k * ee).astype(bf16)
        dec_s[c] = jnp.exp2(t32)

    mask_leaf, mask3, mask2, mask1 = _chunk_masks()
    heads = [slice(h * HEAD_DIM, (h + 1) * HEAD_DIM) for h in range(HEADS)]

    for c in range(n_chunks):
        rows = slice(c * CHUNK, (c + 1) * CHUNK)
        sc = [[_dot_nt(qs_s[l, rows, hs], ks_s[l, rows, hs]) for l in range(4)] for hs in heads]
        sts = [st_s[h] for h in range(HEADS)]
        upd = [_dot_tn(ks_s[4, rows, hs], v_s[rows, hs]) for hs in heads]
        dec = jnp.concatenate([dec_s[c]] * (HEAD_DIM // SUBLANES), axis=0)
        for h, hs in enumerate(heads):
            st_s[h] = sts[h] * dec[:, hs].T + upd[h]
        ps = [jnp.where(mask_leaf, s[0], jnp.where(mask3, s[1], jnp.where(mask2, s[2],
                        jnp.where(mask1, s[3], 0.0)))).astype(bf16) for s in sc]
        for h, hs in enumerate(heads):
            lhs = jnp.concatenate([qs_s[4, rows, hs], ps[h]], axis=1)
            rhs = jnp.concatenate([sts[h].astype(bf16), v_s[rows, hs]], axis=0)
            o_s[rows, hs] = _dot(lhs, rhs)

    for h, hs in enumerate(heads):
        o = o_s[:, hs]
        o = o * lax.rsqrt(jnp.mean(o * o, axis=-1, keepdims=True) + RMS_EPS)
        og_s[:, hs] = (o * ng_ref[:, hs] * gate_s[:, hs]).astype(bf16)

    mix = _dot(og_s[...], wout_ref[...])
    o_ref[...] = _layer_norm(ALPHA * x + mix, lng_ref[...], lnb_ref[...])


def _hgrn_mixer(x, layer_j, w_in, lb_logits, norm_g, w_out, ln_g, ln_b):
    B, S, D = x.shape
    tt = MIX_TILE
    tile = lambda b, t: (b, t, 0)
    full = lambda b, t: (0, 0)
    n_a = lb_logits.shape[0]
    return pl.pallas_call(
        functools.partial(_hgrn_kernel, layer_j),
        out_shape=jax.ShapeDtypeStruct((B, S, D), f32),
        grid=(B, S // tt),
        in_specs=[
            pl.BlockSpec((None, tt, D), tile),
            _layer_weight(layer_j, D, 4 * D, 2),
            pl.BlockSpec((n_a, D), full),
            pl.BlockSpec((1, D), full),
            _layer_weight(layer_j, D, D, 2),
            pl.BlockSpec((1, D), full),
            pl.BlockSpec((1, D), full),
        ],
        out_specs=pl.BlockSpec((None, tt, D), tile),
        scratch_shapes=[
            pltpu.VMEM((tt, D), f32),
            pltpu.VMEM((tt, D), f32),
            pltpu.VMEM((tt, D), bf16),
            pltpu.VMEM((tt, D), f32),
            pltpu.VMEM((tt, D), f32),
            pltpu.VMEM((5, tt, D), bf16),
            pltpu.VMEM((5, tt, D), bf16),
            pltpu.VMEM((tt // CHUNK, SUBLANES, D), f32),
            pltpu.VMEM((tt, D), f32),
            pltpu.VMEM((tt, D), bf16),
            pltpu.VMEM((HEADS, HEAD_DIM, HEAD_DIM), f32),
        ],
        compiler_params=pltpu.CompilerParams(
            dimension_semantics=("arbitrary", "arbitrary"), vmem_limit_bytes=VMEM_LIMIT_BYTES),
        name="hgrn_mixer",
    )(x, w_in, lb_logits, norm_g, w_out, ln_g, ln_b)


def _conv_kernel(x_ref, w1_ref, b1_ref, wdw_ref, bdw_ref, cg_ref, cb_ref, w2_ref, b2_ref,
                 lng_ref, lnb_ref, o_ref, u_s, c_s):
    D = D_MODEL
    tt = x_ref.shape[0]
    n_slabs = D // LANES

    @pl.when(pl.program_id(1) == 0)
    def _():
        u_s[:, 0:CONV_HIST, :] = jnp.zeros((n_slabs, CONV_HIST, LANES), f32)

    x = x_ref[...]
    xb = x.astype(bf16)
    group = 2 * LANES
    for gi in range(D // group):
        cols = slice(gi * group, (gi + 1) * group)
        gcols = slice(D + gi * group, D + (gi + 1) * group)
        a = _dot(xb, w1_ref[:, cols]) + b1_ref[:, cols]
        gate = _dot(xb, w1_ref[:, gcols]) + b1_ref[:, gcols]
        glu = a * jax.nn.sigmoid(gate)
        for s in range(group // LANES):
            u_s[gi * (group // LANES) + s, CONV_HIST:CONV_HIST + tt, :] = glu[:, s * LANES:(s + 1) * LANES]

    base = CONV_HIST - (CONV_WIDTH - 1)
    for s in range(n_slabs):
        cols = slice(s * LANES, (s + 1) * LANES)
        for rb in range(tt // CONV_ROWS):
            acc = jnp.zeros((CONV_ROWS, LANES), f32)
            for j in range(CONV_WIDTH):
                r = rb * CONV_ROWS + base + j
                acc = acc + u_s[s, r:r + CONV_ROWS, :] * wdw_ref[j:j + 1, cols]
            c_s[rb * CONV_ROWS:(rb + 1) * CONV_ROWS, cols] = acc + bdw_ref[:, cols]

    u_s[:, 0:CONV_HIST, :] = u_s[:, tt:tt + CONV_HIST, :]

    u = _layer_norm(c_s[...], cg_ref[...], cb_ref[...])
    u = u * jax.nn.sigmoid(u)
    mix = _dot(u.astype(bf16), w2_ref[...]) + b2_ref[...]
    o_ref[...] = _layer_norm(ALPHA * x + mix, lng_ref[...], lnb_ref[...])


def _conv_mixer(x, layer_j, w1, b1, w_dw, b_dw, cg, cb, w2, b2, ln_g, ln_b):
    B, S, D = x.shape
    tt = MIX_TILE
    tile = lambda b, t: (b, t, 0)
    full = lambda b, t: (0, 0)
    vec = pl.BlockSpec((1, D), full)
    return pl.pallas_call(
        _conv_kernel,
        out_shape=jax.ShapeDtypeStruct((B, S, D), f32),
        grid=(B, S // tt),
        in_specs=[
            pl.BlockSpec((None, tt, D), tile),
            _layer_weight(layer_j, D, 2 * D, 2),
            pl.BlockSpec((1, 2 * D), full),
            pl.BlockSpec((CONV_WIDTH, D), full),
            vec, vec, vec,
            _layer_weight(layer_j, D, D, 2),
            vec, vec, vec,
        ],
        out_specs=pl.BlockSpec((None, tt, D), tile),
        scratch_shapes=[
            pltpu.VMEM((D // LANES, CONV_HIST + tt, LANES), f32),
            pltpu.VMEM((tt, D), f32),
        ],
        compiler_params=pltpu.CompilerParams(
            dimension_semantics=("arbitrary", "arbitrary"), vmem_limit_bytes=VMEM_LIMIT_BYTES),
        name="conv_mixer",
    )(x, w1, b1, w_dw, b_dw, cg, cb, w2, b2, ln_g, ln_b)


def kernel(x, ln_mix_g, ln_mix_b, ln_ffn_g, ln_ffn_b, ffn_w1, ffn_w2, a_w_in, a_lb_logits, a_norm_g, a_w_out, b_w_pw1, b_b_pw1, b_w_dw, b_b_dw, b_ln_g, b_ln_b, b_w_pw2, b_b_pw2):
    B, S, D = x.shape
    assert D == D_MODEL and S % MIX_TILE == 0 and (B * S) % FFN_TILE == 0
    row = lambda a: a.reshape(1, -1)
    lb_logits = a_lb_logits.astype(f32)
    ffn_w1, ffn_w2, a_w_in, a_w_out, b_w_pw1, b_w_pw2 = (
        w.astype(bf16) for w in (ffn_w1, ffn_w2, a_w_in, a_w_out, b_w_pw1, b_w_pw2))
    for i in range(DEPTH):
        j = i // N_MIXERS
        if i % N_MIXERS == 0:
            x = _hgrn_mixer(x, j, a_w_in, lb_logits, row(a_norm_g[j]),
                            a_w_out, row(ln_mix_g[i]), row(ln_mix_b[i]))
        else:
            x = _conv_mixer(x, j, b_w_pw1, row(b_b_pw1[j]), b_w_dw[j], row(b_b_dw[j]),
                            row(b_ln_g[j]), row(b_ln_b[j]), b_w_pw2, row(b_b_pw2[j]),
                            row(ln_mix_g[i]), row(ln_mix_b[i]))
        x = _ffn(x.reshape(B * S, D), i, ffn_w1, ffn_w2,
                 row(ln_ffn_g[i]), row(ln_ffn_b[i])).reshape(B, S, D)
    return x
```
